```python
import jax, jax.numpy as jnp
from jax import lax
import numpy as np

D_MODEL = 4096
BATCH = 1
SEQ = 8192
DEPTH = 2

N_MIXERS = 2
POOL_WINDOWS = (2, 4, 8, 16)
N_POOL_GROUPS = len(POOL_WINDOWS)
POOL_GROUP_DIM = D_MODEL // N_POOL_GROUPS
HEAD_DIM = 128
N_HEADS = D_MODEL // HEAD_DIM
Q_BLOCK = 128
D_FF = 11008
N_EXPERTS = 8
TOP_K = 2
D_FF_EXPERT = 5632
LN_EPS = 1e-5
DEEPNORM_ALPHA = (2 * DEPTH) ** 0.25
DEEPNORM_BETA = (8 * DEPTH) ** -0.25

kernel_name = 'hybrid_pool_stickbreak_moe_deepnorm'


def layer_norm(x, g, b):
    xf = x.astype(jnp.float32)
    mu = jnp.mean(xf, axis=-1, keepdims=True)
    var = jnp.mean(jnp.square(xf - mu), axis=-1, keepdims=True)
    y = (xf - mu) * lax.rsqrt(var + LN_EPS)
    return (y * g.astype(jnp.float32) + b.astype(jnp.float32)).astype(x.dtype)


def pool_mixer(x, w_in, w_group, scale):
    B, S, D = x.shape
    u = (x @ w_in).astype(jnp.float32).reshape(B, S, N_POOL_GROUPS, POOL_GROUP_DIM)
    cs = jnp.cumsum(u, axis=1)
    cs_pad = jnp.concatenate([jnp.zeros((B, 1, N_POOL_GROUPS, POOL_GROUP_DIM), jnp.float32), cs], axis=1)
    pos = jnp.arange(1, S + 1)
    diffs = []
    for g, w in enumerate(POOL_WINDOWS):
        c = cs_pad[:, :, g]
        hi = c[:, 1:]
        lo = jnp.concatenate([jnp.zeros((B, w - 1, POOL_GROUP_DIM), jnp.float32), c[:, :S - w + 1]], axis=1)
        count = jnp.minimum(pos, w).astype(jnp.float32)[None, :, None]
        diffs.append((hi - lo) / count - u[:, :, g])
    d = jnp.stack(diffs, axis=2).astype(x.dtype)
    y = jnp.einsum('bsgc,gcd->bsgd', d, w_group).reshape(B, S, D)
    return y * scale


def stick_breaking_attention(x, w_qkv, w_o):
    B, S, D = x.shape
    qkv = (x @ w_qkv).reshape(B, S, 3, N_HEADS, HEAD_DIM)
    q = qkv[:, :, 0].astype(jnp.float32) * (HEAD_DIM ** -0.5)
    k = qkv[:, :, 1].astype(jnp.float32).transpose(0, 2, 1, 3)
    v = qkv[:, :, 2].astype(jnp.float32).transpose(0, 2, 1, 3)
    nb = S // Q_BLOCK
    q_blocks = q.reshape(B, nb, Q_BLOCK, N_HEADS, HEAD_DIM).transpose(1, 0, 3, 2, 4)
    starts = jnp.arange(nb, dtype=jnp.int32) * Q_BLOCK
    key_pos = jnp.arange(S, dtype=jnp.int32)

    def block(args):
        qi, t0 = args
        z = jnp.einsum('bhqd,bhkd->bhqk', qi, k)
        t = t0 + jnp.arange(Q_BLOCK, dtype=jnp.int32)
        mask = key_pos[None, :] < t[:, None]
        log_keep = jnp.where(mask, jax.nn.log_sigmoid(-z), 0.0)
        suffix = lax.cumsum(log_keep, axis=3, reverse=True) - log_keep
        a = jnp.where(mask, jnp.exp(jax.nn.log_sigmoid(z) + suffix), 0.0)
        return jnp.einsum('bhqk,bhkd->bhqd', a, v)

    o = lax.map(block, (q_blocks, starts))
    o = o.transpose(1, 0, 3, 2, 4).reshape(B, S, D).astype(x.dtype)
    return o @ w_o


def swiglu(h, w_gate, w_up, w_down):
    return (jax.nn.silu(h @ w_gate) * (h @ w_up)) @ w_down


def moe_swiglu(x, w_router, w_gate, w_up, w_down):
    B, S, D = x.shape
    h = x.reshape(B * S, D)
    logits = (h @ w_router).astype(jnp.float32)
    top_vals, top_idx = lax.top_k(logits, TOP_K)
    gates = jax.nn.softmax(top_vals, axis=-1)
    combine = jnp.sum(jax.nn.one_hot(top_idx, N_EXPERTS, dtype=jnp.float32) * gates[..., None], axis=1)
    y = jnp.zeros((B * S, D), jnp.float32)
    for e in range(N_EXPERTS):
        y = y + combine[:, e:e + 1] * swiglu(h, w_gate[e], w_up[e], w_down[e]).astype(jnp.float32)
    return y.reshape(B, S, D).astype(x.dtype)


def setup_inputs(seed: int = 0) -> dict:
    key = jax.random.key(seed)
    ks = jax.random.split(key, 22)
    d = D_MODEL

    def nrm(k, shape, scale):
        return jax.random.normal(k, shape, jnp.float32) * scale

    return {
        'x': nrm(ks[0], (BATCH, SEQ, d), 1.0),
        'l0_pool_w_in': nrm(ks[1], (d, d), d ** -0.5),
        'l0_pool_w_group': nrm(ks[2], (N_POOL_GROUPS, POOL_GROUP_DIM, POOL_GROUP_DIM), POOL_GROUP_DIM ** -0.5 * DEEPNORM_BETA),
        'l0_pool_scale': 1.0 + nrm(ks[3], (d,), 0.02),
        'l0_ln1_g': 1.0 + nrm(ks[4], (d,), 0.02),
        'l0_ln1_b': nrm(ks[5], (d,), 0.02),
        'l0_ffn_w_gate': nrm(ks[6], (d, D_FF), d ** -0.5),
        'l0_ffn_w_up': nrm(ks[7], (d, D_FF), d ** -0.5),
        'l0_ffn_w_down': nrm(ks[8], (D_FF, d), D_FF ** -0.5 * DEEPNORM_BETA),
        'l0_ln2_g': 1.0 + nrm(ks[9], (d,), 0.02),
        'l0_ln2_b': nrm(ks[10], (d,), 0.02),
        'l1_attn_w_qkv': jnp.concatenate([nrm(ks[11], (d, 2 * d), d ** -0.5),
                                          nrm(ks[12], (d, d), d ** -0.5 * DEEPNORM_BETA)], axis=1),
        'l1_attn_w_o': nrm(ks[13], (d, d), d ** -0.5 * DEEPNORM_BETA),
        'l1_ln1_g': 1.0 + nrm(ks[14], (d,), 0.02),
        'l1_ln1_b': nrm(ks[15], (d,), 0.02),
        'l1_moe_w_router': nrm(ks[16], (d, N_EXPERTS), d ** -0.5),
        'l1_moe_w_gate': nrm(ks[17], (N_EXPERTS, d, D_FF_EXPERT), d ** -0.5),
        'l1_moe_w_up': nrm(ks[18], (N_EXPERTS, d, D_FF_EXPERT), d ** -0.5),
        'l1_moe_w_down': nrm(ks[19], (N_EXPERTS, D_FF_EXPERT, d), D_FF_EXPERT ** -0.5 * DEEPNORM_BETA),
        'l1_ln2_g': 1.0 + nrm(ks[20], (d,), 0.02),
        'l1_ln2_b': nrm(ks[21], (d,), 0.02),
    }


def reference(x, l0_pool_w_in, l0_pool_w_group, l0_pool_scale, l0_ln1_g, l0_ln1_b,
              l0_ffn_w_gate, l0_ffn_w_up, l0_ffn_w_down, l0_ln2_g, l0_ln2_b,
              l1_attn_w_qkv, l1_attn_w_o, l1_ln1_g, l1_ln1_b,
              l1_moe_w_router, l1_moe_w_gate, l1_moe_w_up, l1_moe_w_down, l1_ln2_g, l1_ln2_b):
    mixers = [
        lambda h: pool_mixer(h, l0_pool_w_in, l0_pool_w_group, l0_pool_scale),
        lambda h: stick_breaking_attention(h, l1_attn_w_qkv, l1_attn_w_o),
    ]
    channel_mixers = [
        lambda h: swiglu(h, l0_ffn_w_gate, l0_ffn_w_up, l0_ffn_w_down),
        lambda h: moe_swiglu(h, l1_moe_w_router, l1_moe_w_gate, l1_moe_w_up, l1_moe_w_down),
    ]
    ln_mix = [(l0_ln1_g, l0_ln1_b), (l1_ln1_g, l1_ln1_b)]
    ln_ffn = [(l0_ln2_g, l0_ln2_b), (l1_ln2_g, l1_ln2_b)]
    for i in range(DEPTH):
        x = layer_norm(DEEPNORM_ALPHA * x + mixers[i % N_MIXERS](x), *ln_mix[i])
        x = layer_norm(DEEPNORM_ALPHA * x + channel_mixers[i](x), *ln_ffn[i])
    return x
```

```python
import functools

import jax
import jax.numpy as jnp
from jax import lax
from jax.experimental import pallas as pl
from jax.experimental.pallas import tpu as pltpu

HEAD_DIM = 128
POOL_WINDOWS = (2, 4, 8, 16)
POOL_HALO = 16
TOP_K = 2
LN_EPS = 1e-5
DEPTH = 2
ALPHA = (2 * DEPTH) ** 0.25
LANES = 128
FFN_TILE = 512
EPI_COLS = 1024
EPI_ROWS = 128
VMEM_LIMIT = 56 * 1024 * 1024
EXP_UNDERFLOW = -104.0

BF16 = jnp.bfloat16
F32 = jnp.float32


def _params(n_axes):
    return pltpu.CompilerParams(dimension_semantics=("arbitrary",) * n_axes,
                                vmem_limit_bytes=VMEM_LIMIT)


def _layer_norm(t, g, b):
    mu = jnp.mean(t, axis=-1, keepdims=True)
    c = t - mu
    var = jnp.mean(c * c, axis=-1, keepdims=True)
    return c * lax.rsqrt(var + LN_EPS) * g + b


def _mm_kernel(x_ref, w_ref, o_ref, *, scaled_tiles, scale):
    acc = jnp.dot(x_ref[...], w_ref[...], preferred_element_type=F32)
    if scaled_tiles:
        acc = acc * jnp.where(pl.program_id(1) < scaled_tiles, scale, 1.0)
    o_ref[...] = acc.astype(o_ref.dtype)


def matmul(x, w, out_dtype, *, tm=1024, tn=1024, scaled_cols=0, scale=1.0):
    M, K = x.shape
    N = w.shape[1]
    tm, tn = min(tm, M), min(tn, N)
    assert M % tm == 0 and N % tn == 0 and scaled_cols % tn == 0
    return pl.pallas_call(
        functools.partial(_mm_kernel, scaled_tiles=scaled_cols // tn, scale=scale),
        grid=(M // tm, N // tn),
        in_specs=[pl.BlockSpec((tm, K), lambda i, j: (i, 0)),
                  pl.BlockSpec((K, tn), lambda i, j: (0, j))],
        out_specs=pl.BlockSpec((tm, tn), lambda i, j: (i, j)),
        out_shape=jax.ShapeDtypeStruct((M, N), out_dtype),
        compiler_params=_params(2),
        name="matmul",
    )(x, w)


def _mm_ln_kernel(x_ref, w_ref, res_ref, g_ref, b_ref, o_ref, obf_ref, *, nk):
    k = pl.program_id(1)
    tm, D = o_ref.shape
    x = x_ref[...]
    for c in range(0, D, EPI_COLS):
        cs = slice(c, c + EPI_COLS)
        part = jnp.dot(x, w_ref[:, cs], preferred_element_type=F32)

        @pl.when(k == 0)
        def _():
            o_ref[:, cs] = part

        @pl.when(k > 0)
        def _():
            o_ref[:, cs] += part

    @pl.when(k == nk - 1)
    def _():
        for r in range(0, tm, EPI_ROWS):
            rs = slice(r, r + EPI_ROWS)
            y = _layer_norm(ALPHA * res_ref[rs, :] + o_ref[rs, :], g_ref[...], b_ref[...])
            o_ref[rs, :] = y
            obf_ref[rs, :] = y.astype(BF16)


def matmul_residual_ln(x, w, res, g, b, *, tm=512, tk=512):
    M, K = x.shape
    D = w.shape[1]
    tm = min(tm, M)
    assert M % tm == 0 and K % tk == 0
    nk = K // tk
    row = pl.BlockSpec((tm, D), lambda i, k: (i, 0))
    vec = pl.BlockSpec((1, D), lambda i, k: (0, 0))
    return pl.pallas_call(
        functools.partial(_mm_ln_kernel, nk=nk),
        grid=(M // tm, nk),
        in_specs=[pl.BlockSpec((tm, tk), lambda i, k: (i, k)),
                  pl.BlockSpec((tk, D), lambda i, k: (k, 0)),
                  pl.BlockSpec((tm, D), lambda i, k: (i, 0), pipeline_mode=pl.Buffered(1)),
                  vec, vec],
        out_specs=[row, row],
        out_shape=[jax.ShapeDtypeStruct((M, D), F32), jax.ShapeDtypeStruct((M, D), BF16)],
        compiler_params=_params(2),
        name="matmul_residual_ln",
    )(x, w, res, g.reshape(1, D), b.reshape(1, D))


def _pool_kernel(u_ref, halo_ref, x_ref, wg_ref, sc_ref, g_ref, b_ref, o_ref, obf_ref, ext_ref, *, tm, C):
    i = pl.program_id(0)
    pos = i * tm + 1 + lax.broadcasted_iota(jnp.int32, (tm, 1), 0)
    for gi, w in enumerate(POOL_WINDOWS):
        cs = slice(gi * C, (gi + 1) * C)
        ext_ref[0:POOL_HALO, :] = jnp.where(i > 0, halo_ref[:, cs], 0.0)
        ext_ref[POOL_HALO:, :] = u_ref[:, cs]
        u = u_ref[:, cs]
        win = u
        for k in range(1, w):
            win = win + ext_ref[pl.ds(POOL_HALO - k, tm), :]
        count = jnp.minimum(pos, w).astype(F32)
        d = win / count - u
        y = jnp.dot(d.astype(BF16), wg_ref[gi], preferred_element_type=F32)
        o_ref[:, cs] = ALPHA * x_ref[:, cs] + y * sc_ref[:, cs]
    for r in range(0, tm, EPI_ROWS):
        rs = slice(r, r + EPI_ROWS)
        y = _layer_norm(o_ref[rs, :], g_ref[...], b_ref[...])
        o_ref[rs, :] = y
        obf_ref[rs, :] = y.astype(BF16)


def pool_mix_ln(u, x, w_group, scale, g, b, *, tm=256):
    S, D = u.shape
    G, C, _ = w_group.shape
    assert G == len(POOL_WINDOWS) and G * C == D and S % tm == 0 and tm % POOL_HALO == 0
    hb = tm // POOL_HALO
    row = pl.BlockSpec((tm, D), lambda i: (i, 0))
    vec = pl.BlockSpec((1, D), lambda i: (0, 0))
    return pl.pallas_call(
        functools.partial(_pool_kernel, tm=tm, C=C),
        grid=(S // tm,),
        in_specs=[row,
                  pl.BlockSpec((POOL_HALO, D), lambda i: (jnp.maximum(i * hb - 1, 0), 0)),
                  row,
                  pl.BlockSpec((G, C, C), lambda i: (0, 0, 0)),
                  vec, vec, vec],
        out_specs=[row, row],
        out_shape=[jax.ShapeDtypeStruct((S, D), F32), jax.ShapeDtypeStruct((S, D), BF16)],
        scratch_shapes=[pltpu.VMEM((tm + POOL_HALO, C), F32)],
        compiler_params=_params(1),
        name="pool_mix_ln",
    )(u, u, x, w_group, scale.reshape(1, D), g.reshape(1, D), b.reshape(1, D))


def _glu_math(x, wg, wu):
    gate = jnp.dot(x, wg, preferred_element_type=F32)
    up = jnp.dot(x, wu, preferred_element_type=F32)
    return (gate * jax.nn.sigmoid(gate) * up).astype(BF16)


def _glu_kernel(x_ref, wg_ref, wu_ref, o_ref):
    o_ref[...] = _glu_math(x_ref[...], wg_ref[...], wu_ref[...])


def glu(x, wg, wu, *, tm=1024, tf=256):
    M, K = x.shape
    F = wg.shape[1]
    tm = min(tm, M)
    assert M % tm == 0 and F % tf == 0
    wspec = pl.BlockSpec((K, tf), lambda i, j: (0, j))
    return pl.pallas_call(
        _glu_kernel,
        grid=(M // tm, F // tf),
        in_specs=[pl.BlockSpec((tm, K), lambda i, j: (i, 0)), wspec, wspec],
        out_specs=pl.BlockSpec((tm, tf), lambda i, j: (i, j)),
        out_shape=jax.ShapeDtypeStruct((M, F), BF16),
        compiler_params=_params(2),
        name="glu",
    )(x, wg, wu)


def _attn_kernel(q_ref, k_ref, v_ref, tri_ref, o_ref, *, tq, tk):
    i = pl.program_id(1)
    q = q_ref[...]
    tri = tri_ref[...]
    t_pos = i * tq + lax.broadcasted_iota(jnp.int32, (tq, 1), 0)
    lane_pos = lax.broadcasted_iota(jnp.int32, (1, tk), 1)

    def cond(carry):
        j, run, _ = carry
        return jnp.logical_and(j >= 0, jnp.max(run) > EXP_UNDERFLOW)

    def body(carry):
        j, run, acc = carry
        start = pl.multiple_of(j * tk, tk)
        ks = k_ref[pl.ds(start, tk), :]
        vs = v_ref[pl.ds(start, tk), :]
        z = lax.dot_general(q, ks, (((1,), (1,)), ((), ())), preferred_element_type=F32)
        mask = (start + lane_pos) < t_pos
        sp = jnp.log1p(jnp.exp(-jnp.abs(z)))
        log_beta = jnp.minimum(z, 0.0) - sp
        log_keep = jnp.where(mask, -jnp.maximum(z, 0.0) - sp, 0.0)
        hi = log_keep.astype(BF16)
        lo = (log_keep - hi.astype(F32)).astype(BF16)
        suffix = (jnp.dot(hi, tri, preferred_element_type=F32)
                  + jnp.dot(lo, tri, preferred_element_type=F32))
        a = jnp.where(mask, jnp.exp(log_beta + suffix + run), 0.0)
        acc = acc + jnp.dot(a.astype(BF16), vs, preferred_element_type=F32)
        run = run + jnp.sum(log_keep, axis=1, keepdims=True)
        return j - 1, run, acc

    j0 = (i * tq + tq - 1) // tk
    _, _, acc = lax.while_loop(cond, body, (j0, jnp.zeros((tq, 1), F32), jnp.zeros((tq, HEAD_DIM), F32)))
    o_ref[...] = acc.astype(o_ref.dtype)


def stick_breaking_attention(qkv, *, tq=256, tk=256):
    S, D3 = qkv.shape
    D = D3 // 3
    H = D // HEAD_DIM
    assert S % tq == 0 and S % tk == 0
    r = lax.broadcasted_iota(jnp.int32, (tk, tk), 0)
    c = lax.broadcasted_iota(jnp.int32, (tk, tk), 1)
    tri = (r > c).astype(BF16)
    return pl.pallas_call(
        functools.partial(_attn_kernel, tq=tq, tk=tk),
        grid=(H, S // tq),
        in_specs=[pl.BlockSpec((tq, HEAD_DIM), lambda h, i: (i, h)),
                  pl.BlockSpec((S, HEAD_DIM), lambda h, i: (0, H + h)),
                  pl.BlockSpec((S, HEAD_DIM), lambda h, i: (0, 2 * H + h)),
                  pl.BlockSpec((tk, tk), lambda h, i: (0, 0))],
        out_specs=pl.BlockSpec((tq, HEAD_DIM), lambda h, i: (i, h)),
        out_shape=jax.ShapeDtypeStruct((S, D), BF16),
        compiler_params=_params(2),
        name="stick_breaking_attention",
    )(qkv, qkv, qkv, tri)


def _router_kernel(h_ref, w_ref, o_ref, *, n_experts):
    h = h_ref[...]
    w = w_ref[...]
    h1 = h.astype(BF16)
    h2 = (h - h1.astype(F32)).astype(BF16)
    h3 = (h - h1.astype(F32) - h2.astype(F32)).astype(BF16)
    w1 = w.astype(BF16)
    w2 = (w - w1.astype(F32)).astype(BF16)
    w3 = (w - w1.astype(F32) - w2.astype(F32)).astype(BF16)
    dot = functools.partial(jnp.dot, preferred_element_type=F32)
    logits = (dot(h3, w1) + dot(h2, w2) + dot(h1, w3)) + (dot(h2, w1) + dot(h1, w2)) + dot(h1, w1)
    lane = lax.broadcasted_iota(jnp.int32, logits.shape, 1)
    lane_f = lane.astype(F32)
    neg = jnp.float32(-jnp.inf)
    logits = jnp.where(lane < n_experts, logits, neg)
    m1 = jnp.max(logits, axis=1, keepdims=True)
    i1 = jnp.min(jnp.where(logits == m1, lane_f, float(LANES)), axis=1, keepdims=True)
    rest = jnp.where(lane_f == i1, neg, logits)
    m2 = jnp.max(rest, axis=1, keepdims=True)
    i2 = jnp.min(jnp.where(rest == m2, lane_f, float(LANES)), axis=1, keepdims=True)
    e2 = jnp.exp(m2 - m1)
    g1 = 1.0 / (1.0 + e2)
    g2 = e2 / (1.0 + e2)
    out = jnp.where(lane == 0, i1, 0.0)
    out = jnp.where(lane == 1, i2, out)
    out = jnp.where(lane == 2, g1, out)
    out = jnp.where(lane == 3, g2, out)
    o_ref[...] = out


def route_top2(h, w_router, *, tm=256):
    S, D = h.shape
    E = w_router.shape[1]
    assert E <= LANES and S % tm == 0
    w_pad = jnp.pad(w_router, ((0, 0), (0, LANES - E)))
    info = pl.pallas_call(
        functools.partial(_router_kernel, n_experts=E),
        grid=(S // tm,),
        in_specs=[pl.BlockSpec((tm, D), lambda i: (i, 0)),
                  pl.BlockSpec((D, LANES), lambda i: (0, 0))],
        out_specs=pl.BlockSpec((tm, LANES), lambda i: (i, 0)),
        out_shape=jax.ShapeDtypeStruct((S, LANES), F32),
        compiler_params=_params(1),
        name="route_top2",
    )(h, w_pad)
    return info[:, 0:TOP_K].astype(jnp.int32), info[:, TOP_K:2 * TOP_K]


def _gather_kernel(src_ref, h_ref, o_ref, stage_ref, sem, *, tg):
    base = pl.program_id(0) * tg

    def row_copy(r):
        return pltpu.make_async_copy(h_ref.at[pl.ds(src_ref[base + r], 1), :],
                                     stage_ref.at[pl.ds(r, 1), :], sem)

    def issue(r, c):
        row_copy(r).start()
        return c

    def drain(r, c):
        row_copy(r).wait()
        return c

    lax.fori_loop(0, tg, issue, 0)
    lax.fori_loop(0, tg, drain, 0)
    o_ref[...] = stage_ref[...].astype(BF16)


def gather_rows_bf16(h, src_rows, *, tg=256):
    S, D = h.shape
    N = src_rows.shape[0]
    assert N % tg == 0
    return pl.pallas_call(
        functools.partial(_gather_kernel, tg=tg),
        grid_spec=pltpu.PrefetchScalarGridSpec(
            num_scalar_prefetch=1,
            grid=(N // tg,),
            in_specs=[pl.BlockSpec(memory_space=pl.ANY)],
            out_specs=pl.BlockSpec((tg, D), lambda i, src: (i, 0)),
            scratch_shapes=[pltpu.VMEM((tg, D), F32), pltpu.SemaphoreType.DMA(())]),
        out_shape=jax.ShapeDtypeStruct((N, D), BF16),
        compiler_params=_params(1),
        name="gather_rows_bf16",
    )(src_rows, h)


def _expert_glu_kernel(te_ref, used_ref, x_ref, wg_ref, wu_ref, o_ref):
    t = pl.program_id(1)

    @pl.when(t < used_ref[0])
    def _():
        o_ref[...] = _glu_math(x_ref[...], wg_ref[0], wu_ref[0])

    @pl.when(t >= used_ref[0])
    def _():
        o_ref[...] = jnp.zeros_like(o_ref)


def expert_glu(xg, wg, wu, tile_expert, used, *, tm, tf=512):
    N, K = xg.shape
    E, _, F = wg.shape
    assert N % tm == 0 and F % tf == 0
    wspec = pl.BlockSpec((1, K, tf), lambda f, t, te, used: (te[t], 0, f))
    return pl.pallas_call(
        _expert_glu_kernel,
        grid_spec=pltpu.PrefetchScalarGridSpec(
            num_scalar_prefetch=2,
            grid=(F // tf, N // tm),
            in_specs=[pl.BlockSpec((tm, K), lambda f, t, te, used: (t, 0)), wspec, wspec],
            out_specs=pl.BlockSpec((tm, tf), lambda f, t, te, used: (t, f))),
        out_shape=jax.ShapeDtypeStruct((N, F), BF16),
        compiler_params=_params(2),
        name="expert_glu",
    )(tile_expert, used, xg, wg, wu)


def _expert_down_kernel(te_ref, used_ref, x_ref, w_ref, o_ref):
    t = pl.program_id(1)

    @pl.when(t < used_ref[0])
    def _():
        o_ref[...] = jnp.dot(x_ref[...], w_ref[0], preferred_element_type=F32)

    @pl.when(t >= used_ref[0])
    def _():
        o_ref[...] = jnp.zeros_like(o_ref)


def expert_down(hid, wd, tile_expert, used, *, tm, tn=512):
    N, F = hid.shape
    E, _, D = wd.shape
    assert N % tm == 0 and D % tn == 0
    return pl.pallas_call(
        _expert_down_kernel,
        grid_spec=pltpu.PrefetchScalarGridSpec(
            num_scalar_prefetch=2,
            grid=(D // tn, N // tm),
            in_specs=[pl.BlockSpec((tm, F), lambda n, t, te, used: (t, 0)),
                      pl.BlockSpec((1, F, tn), lambda n, t, te, used: (te[t], 0, n))],
            out_specs=pl.BlockSpec((tm, tn), lambda n, t, te, used: (t, n))),
        out_shape=jax.ShapeDtypeStruct((N, D), F32),
        compiler_params=_params(2),
        name="expert_down",
    )(tile_expert, used, hid, wd)


def _combine_kernel(pos_ref, y_ref, h_ref, gate_ref, g_ref, b_ref, o_ref, ya_ref, yb_ref, sem, *, tc):
    base = pl.program_id(0) * tc

    def row_copy(r, k, dst_ref):
        return pltpu.make_async_copy(y_ref.at[pl.ds(pos_ref[(base + r) * TOP_K + k], 1), :],
                                     dst_ref.at[pl.ds(r, 1), :], sem)

    def issue(r, c):
        row_copy(r, 0, ya_ref).start()
        row_copy(r, 1, yb_ref).start()
        return c

    def drain(r, c):
        row_copy(r, 0, ya_ref).wait()
        row_copy(r, 1, yb_ref).wait()
        return c

    lax.fori_loop(0, tc, issue, 0)
    lax.fori_loop(0, tc, drain, 0)
    gates = gate_ref[...]
    moe = gates[:, 0:1] * ya_ref[...] + gates[:, 1:2] * yb_ref[...]
    o_ref[...] = _layer_norm(ALPHA * h_ref[...] + moe, g_ref[...], b_ref[...])


def combine_ln(y, pos, h, gates, g, b, *, tc=256):
    S, D = h.shape
    assert S % tc == 0
    row = pl.BlockSpec((tc, D), lambda i, pos: (i, 0))
    vec = pl.BlockSpec((1, D), lambda i, pos: (0, 0))
    return pl.pallas_call(
        functools.partial(_combine_kernel, tc=tc),
        grid_spec=pltpu.PrefetchScalarGridSpec(
            num_scalar_prefetch=1,
            grid=(S // tc,),
            in_specs=[pl.BlockSpec(memory_space=pl.ANY), row,
                      pl.BlockSpec((tc, TOP_K), lambda i, pos: (i, 0)), vec, vec],
            out_specs=row,
            scratch_shapes=[pltpu.VMEM((tc, D), F32), pltpu.VMEM((tc, D), F32),
                            pltpu.SemaphoreType.DMA(())]),
        out_shape=jax.ShapeDtypeStruct((S, D), F32),
        compiler_params=_params(1),
        name="combine_ln",
    )(pos.reshape(-1), y, h, gates, g.reshape(1, D), b.reshape(1, D))


def _routing_tables(idx, n_experts, tm):
    S, K = idx.shape
    flat = idx.reshape(-1)
    onehot = (flat[:, None] == jnp.arange(n_experts, dtype=jnp.int32)[None, :]).astype(jnp.int32)
    csum = jnp.cumsum(onehot, axis=0)
    rank = jnp.sum(onehot * csum, axis=1) - 1
    counts = csum[-1]
    padded = (counts + tm - 1) // tm * tm
    ends = jnp.cumsum(padded)
    starts = ends - padded
    pos = jnp.sum(onehot * starts[None, :], axis=1) + rank
    n_rows = S * K + n_experts * tm
    src = jnp.zeros((n_rows,), jnp.int32).at[pos].set(jnp.arange(S * K, dtype=jnp.int32) // K)
    used = ends[-1] // tm
    tile_start = jnp.minimum(jnp.arange(n_rows // tm, dtype=jnp.int32), used - 1) * tm
    tile_expert = jnp.sum((tile_start[:, None] >= ends[None, :]).astype(jnp.int32), axis=1)
    return pos.reshape(S, K), src, tile_expert.astype(jnp.int32), used.reshape(1).astype(jnp.int32)


def moe_ln(h, w_router, wg, wu, wd, g, b, *, tm=512):
    E = w_router.shape[1]
    idx, gates = route_top2(h, w_router)
    pos, src, tile_expert, used = _routing_tables(idx, E, tm)
    xg = gather_rows_bf16(h, src)
    hid = expert_glu(xg, wg, wu, tile_expert, used, tm=tm)
    y = expert_down(hid, wd, tile_expert, used, tm=tm)
    return combine_ln(y, pos, h, gates, g, b)


def kernel(x, l0_pool_w_in, l0_pool_w_group, l0_pool_scale, l0_ln1_g, l0_ln1_b, l0_ffn_w_gate, l0_ffn_w_up, l0_ffn_w_down, l0_ln2_g, l0_ln2_b, l1_attn_w_qkv, l1_attn_w_o, l1_ln1_g, l1_ln1_b, l1_moe_w_router, l1_moe_w_gate, l1_moe_w_up, l1_moe_w_down, l1_ln2_g, l1_ln2_b):
    B, S, D = x.shape
    x0 = x.reshape(B * S, D)
    assert B == 1, "sequence mixing below treats all rows as one sequence"
    bf = lambda w: w.astype(BF16)

    u = matmul(bf(x0), bf(l0_pool_w_in), F32)
    x1, x1b = pool_mix_ln(u, x0, bf(l0_pool_w_group), l0_pool_scale, l0_ln1_g, l0_ln1_b)
    fpad = -l0_ffn_w_gate.shape[1] % FFN_TILE
    hid = glu(x1b, jnp.pad(bf(l0_ffn_w_gate), ((0, 0), (0, fpad))),
              jnp.pad(bf(l0_ffn_w_up), ((0, 0), (0, fpad))), tf=FFN_TILE)
    x2, x2b = matmul_residual_ln(hid, jnp.pad(bf(l0_ffn_w_down), ((0, fpad), (0, 0))), x1,
                                 l0_ln2_g, l0_ln2_b, tk=FFN_TILE)

    qkv = matmul(x2b, bf(l1_attn_w_qkv), BF16, scaled_cols=D, scale=HEAD_DIM ** -0.5)
    o = stick_breaking_attention(qkv)
    x3, _ = matmul_residual_ln(o, bf(l1_attn_w_o), x2, l1_ln1_g, l1_ln1_b)
    x4 = moe_ln(x3, l1_moe_w_router, bf(l1_moe_w_gate), bf(l1_moe_w_up), bf(l1_moe_w_down),
                l1_ln2_g, l1_ln2_b)
    return x4.reshape(B, S, D)
```

```python
import functools

import jax
import jax.numpy as jnp
from jax import lax
from jax.experimental import pallas as pl
from jax.experimental.pallas import tpu as pltpu

HEAD_DIM = 128
POOL_WINDOWS = (2, 4, 8, 16)
POOL_HALO = 16
TOP_K = 2
LN_EPS = 1e-5
DEPTH = 2
ALPHA = (2 * DEPTH) ** 0.25
LANES = 128
DENSE_ROWS = 1024
DENSE_FF_TILE = 256
EXPERT_FF_TILE = 512
DMA_UNROLL = 8
EPI_COLS = 1024
EPI_ROWS = 128
VMEM_LIMIT = 56 * 1024 * 1024
EXP_UNDERFLOW = -104.0

BF16 = jnp.bfloat16
F32 = jnp.float32


def _params(n_axes):
    return pltpu.CompilerParams(dimension_semantics=("arbitrary",) * n_axes,
                                vmem_limit_bytes=VMEM_LIMIT)


def _layer_norm(t, g, b):
    mu = jnp.mean(t, axis=-1, keepdims=True)
    c = t - mu
    var = jnp.mean(c * c, axis=-1, keepdims=True)
    return c * lax.rsqrt(var + LN_EPS) * g + b


def _mm_kernel(x_ref, w_ref, o_ref, *, scaled_tiles, scale):
    acc = jnp.dot(x_ref[...], w_ref[...], preferred_element_type=F32)
    if scaled_tiles:
        acc = acc * jnp.where(pl.program_id(1) < scaled_tiles, scale, 1.0)
    o_ref[...] = acc.astype(o_ref.dtype)


def matmul(x, w, out_dtype, *, tm=1024, tn=1024, scaled_cols=0, scale=1.0):
    M, K = x.shape
    N = w.shape[1]
    tm, tn = min(tm, M), min(tn, N)
    assert M % tm == 0 and N % tn == 0 and scaled_cols % tn == 0
    return pl.pallas_call(
        functools.partial(_mm_kernel, scaled_tiles=scaled_cols // tn, scale=scale),
        grid=(M // tm, N // tn),
        in_specs=[pl.BlockSpec((tm, K), lambda i, j: (i, 0)),
                  pl.BlockSpec((K, tn), lambda i, j: (0, j))],
        out_specs=pl.BlockSpec((tm, tn), lambda i, j: (i, j)),
        out_shape=jax.ShapeDtypeStruct((M, N), out_dtype),
        compiler_params=_params(2),
        name="matmul",
    )(x, w)


def _mm_ln_kernel(x_ref, w_ref, res_ref, g_ref, b_ref, o_ref, obf_ref, *, nk, k_tail):
    k = pl.program_id(1)
    tm, D = o_ref.shape
    tk = x_ref.shape[1]

    @pl.when(k == 0)
    def _():
        o_ref[...] = jnp.zeros_like(o_ref)

    def accumulate(valid):
        x = x_ref[...]
        if valid < tk:
            keep_x = lax.broadcasted_iota(jnp.int32, (1, tk), 1) < valid
            keep_w = lax.broadcasted_iota(jnp.int32, (tk, 1), 0) < valid
            x = jnp.where(keep_x, x.astype(F32), 0.0).astype(BF16)
        for c in range(0, D, EPI_COLS):
            cs = slice(c, c + EPI_COLS)
            w = w_ref[:, cs]
            if valid < tk:
                w = jnp.where(keep_w, w.astype(F32), 0.0).astype(BF16)
            o_ref[:, cs] += jnp.dot(x, w, preferred_element_type=F32)

    if k_tail == tk:
        accumulate(tk)
    else:
        pl.when(k < nk - 1)(lambda: accumulate(tk))
        pl.when(k == nk - 1)(lambda: accumulate(k_tail))

    @pl.when(k == nk - 1)
    def _():
        for r in range(0, tm, EPI_ROWS):
            rs = slice(r, r + EPI_ROWS)
            y = _layer_norm(ALPHA * res_ref[rs, :] + o_ref[rs, :], g_ref[...], b_ref[...])
            o_ref[rs, :] = y
            obf_ref[rs, :] = y.astype(BF16)


def matmul_residual_ln(x, w, res, g, b, *, tm=512, tk=512):
    M, K = x.shape
    D = w.shape[1]
    tm = min(tm, M)
    assert M % tm == 0
    nk = pl.cdiv(K, tk)
    row = pl.BlockSpec((tm, D), lambda i, k: (i, 0))
    vec = pl.BlockSpec((1, D), lambda i, k: (0, 0))
    return pl.pallas_call(
        functools.partial(_mm_ln_kernel, nk=nk, k_tail=K - (nk - 1) * tk),
        grid=(M // tm, nk),
        in_specs=[pl.BlockSpec((tm, tk), lambda i, k: (i, k)),
                  pl.BlockSpec((tk, D), lambda i, k: (k, 0)),
                  pl.BlockSpec((tm, D), lambda i, k: (i, 0), pipeline_mode=pl.Buffered(1)),
                  vec, vec],
        out_specs=[row, row],
        out_shape=[jax.ShapeDtypeStruct((M, D), F32), jax.ShapeDtypeStruct((M, D), BF16)],
        compiler_params=_params(2),
        name="matmul_residual_ln",
    )(x, w, res, g.reshape(1, D), b.reshape(1, D))


def _pool_kernel(u_ref, halo_ref, x_ref, wg_ref, sc_ref, g_ref, b_ref, o_ref, obf_ref, ext_ref, *, tm, C):
    i = pl.program_id(0)
    pos = i * tm + 1 + lax.broadcasted_iota(jnp.int32, (tm, 1), 0)
    for gi, w in enumerate(POOL_WINDOWS):
        cs = slice(gi * C, (gi + 1) * C)
        ext_ref[0:POOL_HALO, :] = jnp.where(i > 0, halo_ref[:, cs], 0.0)
        ext_ref[POOL_HALO:, :] = u_ref[:, cs]
        u = u_ref[:, cs]
        win = u
        for k in range(1, w):
            win = win + ext_ref[pl.ds(POOL_HALO - k, tm), :]
        count = jnp.minimum(pos, w).astype(F32)
        d = win / count - u
        y = jnp.dot(d.astype(BF16), wg_ref[gi], preferred_element_type=F32)
        o_ref[:, cs] = ALPHA * x_ref[:, cs] + y * sc_ref[:, cs]
    for r in range(0, tm, EPI_ROWS):
        rs = slice(r, r + EPI_ROWS)
        y = _layer_norm(o_ref[rs, :], g_ref[...], b_ref[...])
        o_ref[rs, :] = y
        obf_ref[rs, :] = y.astype(BF16)


def pool_mix_ln(u, x, w_group, scale, g, b, *, tm=256):
    S, D = u.shape
    G, C, _ = w_group.shape
    assert G == len(POOL_WINDOWS) and G * C == D and S % tm == 0 and tm % POOL_HALO == 0
    hb = tm // POOL_HALO
    row = pl.BlockSpec((tm, D), lambda i: (i, 0))
    vec = pl.BlockSpec((1, D), lambda i: (0, 0))
    return pl.pallas_call(
        functools.partial(_pool_kernel, tm=tm, C=C),
        grid=(S // tm,),
        in_specs=[row,
                  pl.BlockSpec((POOL_HALO, D), lambda i: (jnp.maximum(i * hb - 1, 0), 0)),
                  row,
                  pl.BlockSpec((G, C, C), lambda i: (0, 0, 0)),
                  vec, vec, vec],
        out_specs=[row, row],
        out_shape=[jax.ShapeDtypeStruct((S, D), F32), jax.ShapeDtypeStruct((S, D), BF16)],
        scratch_shapes=[pltpu.VMEM((tm + POOL_HALO, C), F32)],
        compiler_params=_params(1),
        name="pool_mix_ln",
    )(u, u, x, w_group, scale.reshape(1, D), g.reshape(1, D), b.reshape(1, D))


def _glu_math(x, wg, wu):
    gate = jnp.dot(x, wg, preferred_element_type=F32)
    up = jnp.dot(x, wu, preferred_element_type=F32)
    return (gate * jax.nn.sigmoid(gate) * up).astype(BF16)


def _sb_tile(q, ks, vs, tri2, run, mask):
    z = lax.dot_general(q, ks, (((1,), (1,)), ((), ())), preferred_element_type=F32)
    sp = jnp.log(1.0 + jnp.exp(-jnp.abs(z)))
    log_beta = jnp.minimum(z, 0.0) - sp
    log_keep = -jnp.maximum(z, 0.0) - sp
    if mask is not None:
        log_keep = jnp.where(mask, log_keep, 0.0)
    hi = log_keep.astype(BF16)
    lo = (log_keep - hi.astype(F32)).astype(BF16)
    suffix = jnp.dot(jnp.concatenate([hi, lo], axis=1), tri2, preferred_element_type=F32)
    w = jnp.exp(log_beta + suffix + run)
    if mask is not None:
        w = jnp.where(mask, w, 0.0)
    pv = jnp.dot(w.astype(BF16), vs, preferred_element_type=F32)
    return pv, jnp.sum(log_keep, axis=1, keepdims=True)


def _attn_kernel(q_ref, k_ref, v_ref, tri_win_ref, tri_tile_ref, o_ref, *, tq, heads):
    i = pl.program_id(1)
    t0 = i * tq
    w0 = pl.multiple_of(jnp.maximum(t0 - tq, 0), tq)
    t_pos = t0 + lax.broadcasted_iota(jnp.int32, (tq, 1), 0)
    mask = (w0 + lax.broadcasted_iota(jnp.int32, (1, 2 * tq), 1)) < t_pos
    zero = jnp.zeros((tq, 1), F32)
    accs, runs = [], []
    for g in range(heads):
        hs = slice(g * HEAD_DIM, (g + 1) * HEAD_DIM)
        pv, rs = _sb_tile(q_ref[:, hs], k_ref[pl.ds(w0, 2 * tq), hs], v_ref[pl.ds(w0, 2 * tq), hs],
                          tri_win_ref[...], zero, mask)
        accs.append(pv)
        runs.append(rs)

    def alive(runs):
        return functools.reduce(jnp.maximum, [jnp.max(r) for r in runs]) > EXP_UNDERFLOW

    def cond(carry):
        j, runs, _ = carry
        return jnp.logical_and(j >= 0, alive(runs))

    def body(carry):
        j, runs, accs = carry
        start = pl.multiple_of(j * tq, tq)
        new_runs, new_accs = [], []
        for g in range(heads):
            hs = slice(g * HEAD_DIM, (g + 1) * HEAD_DIM)
            pv, rs = _sb_tile(q_ref[:, hs], k_ref[pl.ds(start, tq), hs], v_ref[pl.ds(start, tq), hs],
                              tri_tile_ref[...], runs[g], None)
            new_accs.append(accs[g] + pv)
            new_runs.append(runs[g] + rs)
        return j - 1, tuple(new_runs), tuple(new_accs)

    _, _, accs = lax.while_loop(cond, body, (w0 // tq - 1, tuple(runs), tuple(accs)))
    for g in range(heads):
        o_ref[:, g * HEAD_DIM:(g + 1) * HEAD_DIM] = accs[g].astype(o_ref.dtype)


def _stacked_tri(n):
    r = lax.broadcasted_iota(jnp.int32, (n, n), 0)
    c = lax.broadcasted_iota(jnp.int32, (n, n), 1)
    tri = (r > c).astype(BF16)
    return jnp.concatenate([tri, tri], axis=0)


def stick_breaking_attention(qkv, *, tq=256, heads=2):
    S, D3 = qkv.shape
    D = D3 // 3
    H = D // HEAD_DIM
    assert S % tq == 0 and S >= 2 * tq and H % heads == 0
    hw = heads * HEAD_DIM
    nb = D // hw
    return pl.pallas_call(
        functools.partial(_attn_kernel, tq=tq, heads=heads),
        grid=(nb, S // tq),
        in_specs=[pl.BlockSpec((tq, hw), lambda h, i: (i, h)),
                  pl.BlockSpec((S, hw), lambda h, i: (0, nb + h)),
                  pl.BlockSpec((S, hw), lambda h, i: (0, 2 * nb + h)),
                  pl.BlockSpec((4 * tq, 2 * tq), lambda h, i: (0, 0)),
                  pl.BlockSpec((2 * tq, tq), lambda h, i: (0, 0))],
        out_specs=pl.BlockSpec((tq, hw), lambda h, i: (i, h)),
        out_shape=jax.ShapeDtypeStruct((S, D), BF16),
        compiler_params=_params(2),
        name="stick_breaking_attention",
    )(qkv, qkv, qkv, _stacked_tri(2 * tq), _stacked_tri(tq))


def _router_kernel(h_ref, w_ref, o_ref, *, n_experts):
    h = h_ref[...]
    w = w_ref[...]
    h1 = h.astype(BF16)
    h2 = (h - h1.astype(F32)).astype(BF16)
    h3 = (h - h1.astype(F32) - h2.astype(F32)).astype(BF16)
    w1 = w.astype(BF16)
    w2 = (w - w1.astype(F32)).astype(BF16)
    w3 = (w - w1.astype(F32) - w2.astype(F32)).astype(BF16)
    dot = functools.partial(jnp.dot, preferred_element_type=F32)
    logits = (dot(h3, w1) + dot(h2, w2) + dot(h1, w3)) + (dot(h2, w1) + dot(h1, w2)) + dot(h1, w1)
    lane = lax.broadcasted_iota(jnp.int32, logits.shape, 1)
    lane_f = lane.astype(F32)
    neg = jnp.float32(-jnp.inf)
    logits = jnp.where(lane < n_experts, logits, neg)
    m1 = jnp.max(logits, axis=1, keepdims=True)
    i1 = jnp.min(jnp.where(logits == m1, lane_f, float(LANES)), axis=1, keepdims=True)
    rest = jnp.where(lane_f == i1, neg, logits)
    m2 = jnp.max(rest, axis=1, keepdims=True)
    i2 = jnp.min(jnp.where(rest == m2, lane_f, float(LANES)), axis=1, keepdims=True)
    e2 = jnp.exp(m2 - m1)
    g1 = 1.0 / (1.0 + e2)
    g2 = e2 / (1.0 + e2)
    out = jnp.where(lane == 0, i1, 0.0)
    out = jnp.where(lane == 1, i2, out)
    out = jnp.where(lane == 2, g1, out)
    out = jnp.where(lane == 3, g2, out)
    o_ref[...] = out


def route_top2(h, w_router, *, tm=256):
    S, D = h.shape
    E = w_router.shape[1]
    assert E <= LANES and S % tm == 0
    w_pad = jnp.pad(w_router, ((0, 0), (0, LANES - E)))
    info = pl.pallas_call(
        functools.partial(_router_kernel, n_experts=E),
        grid=(S // tm,),
        in_specs=[pl.BlockSpec((tm, D), lambda i: (i, 0)),
                  pl.BlockSpec((D, LANES), lambda i: (0, 0))],
        out_specs=pl.BlockSpec((tm, LANES), lambda i: (i, 0)),
        out_shape=jax.ShapeDtypeStruct((S, LANES), F32),
        compiler_params=_params(1),
        name="route_top2",
    )(h, w_pad)
    return info[:, 0:TOP_K].astype(jnp.int32), info[:, TOP_K:2 * TOP_K]


def _gather_kernel(src_ref, live_ref, h_ref, o_ref, stage_ref, sems, *, tg):
    i = pl.program_id(0)
    n_live = live_ref[0] // tg
    slot = i % 2

    def row_copy(src_row, r, slot):
        return pltpu.make_async_copy(h_ref.at[pl.ds(src_row, 1), :],
                                     stage_ref.at[slot, pl.ds(r, 1), :], sems.at[slot])

    def issue(step, slot):
        def start_row(r, c):
            row_copy(src_ref[step * tg + r], r, slot).start()
            return c
        lax.fori_loop(0, tg, start_row, 0, unroll=DMA_UNROLL)

    @pl.when(jnp.logical_and(i == 0, n_live > 0))
    def _():
        issue(0, 0)

    @pl.when(i + 1 < n_live)
    def _():
        issue(i + 1, 1 - slot)

    @pl.when(i < n_live)
    def _():
        def wait_row(r, c):
            row_copy(0, r, slot).wait()
            return c
        lax.fori_loop(0, tg, wait_row, 0, unroll=DMA_UNROLL)
        o_ref[...] = stage_ref[slot].astype(BF16)

    @pl.when(i >= n_live)
    def _():
        o_ref[...] = jnp.zeros_like(o_ref)


def gather_rows_bf16(h, src_rows, live_rows, *, tg=256):
    S, D = h.shape
    N = src_rows.shape[0]
    assert N % tg == 0
    return pl.pallas_call(
        functools.partial(_gather_kernel, tg=tg),
        grid_spec=pltpu.PrefetchScalarGridSpec(
            num_scalar_prefetch=2,
            grid=(N // tg,),
            in_specs=[pl.BlockSpec(memory_space=pl.ANY)],
            out_specs=pl.BlockSpec((tg, D), lambda i, src, live: (i, 0)),
            scratch_shapes=[pltpu.VMEM((2, tg, D), F32), pltpu.SemaphoreType.DMA((2,))]),
        out_shape=jax.ShapeDtypeStruct((N, D), BF16),
        compiler_params=_params(1),
        name="gather_rows_bf16",
    )(src_rows, live_rows, h)


def _weights_changed(te_ref, t):
    return jnp.logical_or(t == 0, te_ref[t] != te_ref[jnp.maximum(t - 1, 0)])


def _grouped_glu_kernel(te_ref, used_ref, x_ref, wg_ref, wu_ref, o_ref, wgb_ref, wub_ref):
    t = pl.program_id(1)

    @pl.when(_weights_changed(te_ref, t))
    def _():
        wgb_ref[...] = wg_ref[0].astype(BF16)
        wub_ref[...] = wu_ref[0].astype(BF16)

    @pl.when(t < used_ref[0])
    def _():
        o_ref[...] = _glu_math(x_ref[...], wgb_ref[...], wub_ref[...])

    @pl.when(t >= used_ref[0])
    def _():
        o_ref[...] = jnp.zeros_like(o_ref)


def grouped_glu(x, wg, wu, tile_expert, used, *, tm, tf):
    N, K = x.shape
    E, _, F = wg.shape
    assert N % tm == 0 and F % tf == 0
    wspec = pl.BlockSpec((1, K, tf), lambda f, t, te, used: (te[t], 0, f))
    return pl.pallas_call(
        _grouped_glu_kernel,
        grid_spec=pltpu.PrefetchScalarGridSpec(
            num_scalar_prefetch=2,
            grid=(F // tf, N // tm),
            in_specs=[pl.BlockSpec((tm, K), lambda f, t, te, used: (t, 0)), wspec, wspec],
            out_specs=pl.BlockSpec((tm, tf), lambda f, t, te, used: (t, f)),
            scratch_shapes=[pltpu.VMEM((K, tf), BF16), pltpu.VMEM((K, tf), BF16)]),
        out_shape=jax.ShapeDtypeStruct((N, F), BF16),
        compiler_params=_params(2),
        name="grouped_glu",
    )(tile_expert, used, x, wg, wu)


def _expert_down_kernel(te_ref, used_ref, x_ref, w_ref, o_ref, wb_ref):
    t = pl.program_id(1)

    @pl.when(_weights_changed(te_ref, t))
    def _():
        wb_ref[...] = w_ref[0].astype(BF16)

    @pl.when(t < used_ref[0])
    def _():
        o_ref[...] = jnp.dot(x_ref[...], wb_ref[...], preferred_element_type=F32)

    @pl.when(t >= used_ref[0])
    def _():
        o_ref[...] = jnp.zeros_like(o_ref)


def expert_down(hid, wd, tile_expert, used, *, tm, tn=512):
    N, F = hid.shape
    E, _, D = wd.shape
    assert N % tm == 0 and D % tn == 0
    return pl.pallas_call(
        _expert_down_kernel,
        grid_spec=pltpu.PrefetchScalarGridSpec(
            num_scalar_prefetch=2,
            grid=(D // tn, N // tm),
            in_specs=[pl.BlockSpec((tm, F), lambda n, t, te, used: (t, 0)),
                      pl.BlockSpec((1, F, tn), lambda n, t, te, used: (te[t], 0, n))],
            out_specs=pl.BlockSpec((tm, tn), lambda n, t, te, used: (t, n)),
            scratch_shapes=[pltpu.VMEM((F, tn), BF16)]),
        out_shape=jax.ShapeDtypeStruct((N, D), F32),
        compiler_params=_params(2),
        name="expert_down",
    )(tile_expert, used, hid, wd)


def _combine_kernel(pos_ref, y_ref, h_ref, gate_ref, g_ref, b_ref, o_ref, stage_ref, sems, *, tc):
    i = pl.program_id(0)
    n = pl.num_programs(0)
    slot = i % 2

    def row_copy(y_row, r, k, slot):
        return pltpu.make_async_copy(y_ref.at[pl.ds(y_row, 1), :],
                                     stage_ref.at[slot, k, pl.ds(r, 1), :], sems.at[slot])

    def issue(step, slot):
        def start_row(r, c):
            for k in range(TOP_K):
                row_copy(pos_ref[(step * tc + r) * TOP_K + k], r, k, slot).start()
            return c
        lax.fori_loop(0, tc, start_row, 0, unroll=DMA_UNROLL)

    @pl.when(i == 0)
    def _():
        issue(0, 0)

    @pl.when(i + 1 < n)
    def _():
        issue(i + 1, 1 - slot)

    def wait_row(r, c):
        for k in range(TOP_K):
            row_copy(0, r, k, slot).wait()
        return c
    lax.fori_loop(0, tc, wait_row, 0, unroll=DMA_UNROLL)

    for r in range(0, tc, EPI_ROWS):
        rs = slice(r, r + EPI_ROWS)
        gates = gate_ref[rs, :]
        moe = gates[:, 0:1] * stage_ref[slot, 0, rs, :]
        for k in range(1, TOP_K):
            moe = moe + gates[:, k:k + 1] * stage_ref[slot, k, rs, :]
        o_ref[rs, :] = _layer_norm(ALPHA * h_ref[rs, :] + moe, g_ref[...], b_ref[...])


def combine_ln(y, pos, h, gates, g, b, *, tc=256):
    S, D = h.shape
    assert S % tc == 0
    row = pl.BlockSpec((tc, D), lambda i, pos: (i, 0))
    vec = pl.BlockSpec((1, D), lambda i, pos: (0, 0))
    return pl.pallas_call(
        functools.partial(_combine_kernel, tc=tc),
        grid_spec=pltpu.PrefetchScalarGridSpec(
            num_scalar_prefetch=1,
            grid=(S // tc,),
            in_specs=[pl.BlockSpec(memory_space=pl.ANY), row,
                      pl.BlockSpec((tc, TOP_K), lambda i, pos: (i, 0)), vec, vec],
            out_specs=row,
            scratch_shapes=[pltpu.VMEM((2, TOP_K, tc, D), F32), pltpu.SemaphoreType.DMA((2,))]),
        out_shape=jax.ShapeDtypeStruct((S, D), F32),
        compiler_params=_params(1),
        name="combine_ln",
    )(pos.reshape(-1), y, h, gates, g.reshape(1, D), b.reshape(1, D))


def _routing_tables(idx, n_experts, tm):
    S, K = idx.shape
    flat = idx.reshape(-1)
    onehot = (flat[:, None] == jnp.arange(n_experts, dtype=jnp.int32)[None, :]).astype(jnp.int32)
    csum = jnp.cumsum(onehot, axis=0)
    rank = jnp.sum(onehot * csum, axis=1) - 1
    counts = csum[-1]
    padded = (counts + tm - 1) // tm * tm
    ends = jnp.cumsum(padded)
    starts = ends - padded
    pos = jnp.sum(onehot * starts[None, :], axis=1) + rank
    n_rows = S * K + n_experts * tm
    src = jnp.zeros((n_rows,), jnp.int32).at[pos].set(jnp.arange(S * K, dtype=jnp.int32) // K)
    used = ends[-1] // tm
    tile_start = jnp.minimum(jnp.arange(n_rows // tm, dtype=jnp.int32), used - 1) * tm
    tile_expert = jnp.sum((tile_start[:, None] >= ends[None, :]).astype(jnp.int32), axis=1)
    return pos.reshape(S, K), src, tile_expert.astype(jnp.int32), used.reshape(1).astype(jnp.int32)


def moe_ln(h, w_router, wg, wu, wd, g, b, *, tm=512):
    E = w_router.shape[1]
    idx, gates = route_top2(h, w_router)
    pos, src, tile_expert, used = _routing_tables(idx, E, tm)
    xg = gather_rows_bf16(h, src, used * tm)
    hid = grouped_glu(xg, wg, wu, tile_expert, used, tm=tm, tf=EXPERT_FF_TILE)
    y = expert_down(hid, wd, tile_expert, used, tm=tm)
    return combine_ln(y, pos, h, gates, g, b)


def kernel(x, l0_pool_w_in, l0_pool_w_group, l0_pool_scale, l0_ln1_g, l0_ln1_b, l0_ffn_w_gate, l0_ffn_w_up, l0_ffn_w_down, l0_ln2_g, l0_ln2_b, l1_attn_w_qkv, l1_attn_w_o, l1_ln1_g, l1_ln1_b, l1_moe_w_router, l1_moe_w_gate, l1_moe_w_up, l1_moe_w_down, l1_ln2_g, l1_ln2_b):
    B, S, D = x.shape
    x0 = x.reshape(B * S, D)
    assert B == 1, "sequence mixing below treats all rows as one sequence"
    bf = lambda w: w.astype(BF16)

    u = matmul(bf(x0), bf(l0_pool_w_in), F32)
    x1, x1b = pool_mix_ln(u, x0, bf(l0_pool_w_group), l0_pool_scale, l0_ln1_g, l0_ln1_b)
    n_tiles = (B * S) // DENSE_ROWS
    hid = grouped_glu(x1b, l0_ffn_w_gate[None], l0_ffn_w_up[None], jnp.zeros((n_tiles,), jnp.int32),
                      jnp.full((1,), n_tiles, jnp.int32), tm=DENSE_ROWS, tf=DENSE_FF_TILE)
    x2, x2b = matmul_residual_ln(hid, bf(l0_ffn_w_down), x1, l0_ln2_g, l0_ln2_b)

    qkv = matmul(x2b, bf(l1_attn_w_qkv), BF16, scaled_cols=D, scale=HEAD_DIM ** -0.5)
    o = stick_breaking_attention(qkv)
    x3, _ = matmul_residual_ln(o, bf(l1_attn_w_o), x2, l1_ln1_g, l1_ln1_b)
    x4 = moe_ln(x3, l1_moe_w_router, l1_moe_w_gate, l1_moe_w_up, l1_moe_w_down, l1_ln2_g, l1_ln2_b)
    return x4.reshape(B, S, D)
```

```python
import functools

import jax
import jax.numpy as jnp
from jax import lax
from jax.experimental import pallas as pl
from jax.experimental.pallas import tpu as pltpu

HEAD_DIM = 128
POOL_WINDOWS = (2, 4, 8, 16)
POOL_HALO = 16
TOP_K = 2
LN_EPS = 1e-5
DEPTH = 2
ALPHA = (2 * DEPTH) ** 0.25
LANES = 128
DENSE_ROWS = 1024
DENSE_FF_TILE = 256
EXPERT_FF_TILE = 512
DMA_UNROLL = 8
EPI_COLS = 1024
EPI_ROWS = 128
VMEM_LIMIT = 56 * 1024 * 1024
EXP_UNDERFLOW = -104.0
MASKED_LOGIT = -1e30

BF16 = jnp.bfloat16
F32 = jnp.float32


def _params(n_axes):
    return pltpu.CompilerParams(dimension_semantics=("arbitrary",) * n_axes,
                                vmem_limit_bytes=VMEM_LIMIT)


def _layer_norm(t, g, b):
    mu = jnp.mean(t, axis=-1, keepdims=True)
    c = t - mu
    var = jnp.mean(c * c, axis=-1, keepdims=True)
    return c * lax.rsqrt(var + LN_EPS) * g + b


def _mm_kernel(x_ref, w_ref, o_ref, wb_ref, *, scaled_tiles, scale):
    @pl.when(pl.program_id(1) == 0)
    def _():
        wb_ref[...] = w_ref[...].astype(BF16)

    acc = jnp.dot(x_ref[...], wb_ref[...], preferred_element_type=F32)
    if scaled_tiles:
        acc = acc * jnp.where(pl.program_id(0) < scaled_tiles, scale, 1.0)
    o_ref[...] = acc.astype(o_ref.dtype)


def matmul(x, w, out_dtype, *, tm=1024, tn=512, scaled_cols=0, scale=1.0):
    M, K = x.shape
    N = w.shape[1]
    tm, tn = min(tm, M), min(tn, N)
    assert M % tm == 0 and N % tn == 0 and scaled_cols % tn == 0
    return pl.pallas_call(
        functools.partial(_mm_kernel, scaled_tiles=scaled_cols // tn, scale=scale),
        grid=(N // tn, M // tm),
        in_specs=[pl.BlockSpec((tm, K), lambda j, i: (i, 0)),
                  pl.BlockSpec((K, tn), lambda j, i: (0, j))],
        out_specs=pl.BlockSpec((tm, tn), lambda j, i: (i, j)),
        out_shape=jax.ShapeDtypeStruct((M, N), out_dtype),
        scratch_shapes=[pltpu.VMEM((K, tn), BF16)],
        compiler_params=_params(2),
        name="matmul",
    )(x, w)


def _mm_ln_kernel(x_ref, w_ref, res_hbm, g_ref, b_ref, o_ref, *rest, nk, k_tail):
    obf_ref = rest[0] if len(rest) == 3 else None
    res_ref, res_sem = rest[-2:]
    i, k = pl.program_id(0), pl.program_id(1)
    tm, D = o_ref.shape
    tk = x_ref.shape[1]
    res_copy = pltpu.make_async_copy(res_hbm.at[pl.ds(pl.multiple_of(i * tm, tm), tm), :], res_ref, res_sem)

    @pl.when(k == 0)
    def _():
        res_copy.start()
        o_ref[...] = jnp.zeros_like(o_ref)

    def accumulate(valid):
        x = x_ref[...]
        if valid < tk:
            keep_x = lax.broadcasted_iota(jnp.int32, (1, tk), 1) < valid
            keep_w = lax.broadcasted_iota(jnp.int32, (tk, 1), 0) < valid
            x = jnp.where(keep_x, x.astype(F32), 0.0).astype(BF16)
        for c in range(0, D, EPI_COLS):
            cs = slice(c, c + EPI_COLS)
            w = w_ref[:, cs]
            if valid < tk:
                w = jnp.where(keep_w, w.astype(F32), 0.0)
            o_ref[:, cs] += jnp.dot(x, w.astype(BF16), preferred_element_type=F32)

    if k_tail == tk:
        accumulate(tk)
    else:
        pl.when(k < nk - 1)(lambda: accumulate(tk))
        pl.when(k == nk - 1)(lambda: accumulate(k_tail))

    @pl.when(k == nk - 1)
    def _():
        res_copy.wait()
        for r in range(0, tm, EPI_ROWS):
            rs = slice(r, r + EPI_ROWS)
            y = _layer_norm(ALPHA * res_ref[rs, :] + o_ref[rs, :], g_ref[...], b_ref[...])
            o_ref[rs, :] = y
            if obf_ref is not None:
                obf_ref[rs, :] = y.astype(BF16)


def matmul_residual_ln(x, w, res, g, b, *, emit_bf16, tm=512, tk=512):
    M, K = x.shape
    D = w.shape[1]
    tm = min(tm, M)
    assert M % tm == 0
    nk = pl.cdiv(K, tk)
    row = pl.BlockSpec((tm, D), lambda i, k: (i, 0))
    vec = pl.BlockSpec((1, D), lambda i, k: (0, 0))
    return pl.pallas_call(
        functools.partial(_mm_ln_kernel, nk=nk, k_tail=K - (nk - 1) * tk),
        grid=(M // tm, nk),
        in_specs=[pl.BlockSpec((tm, tk), lambda i, k: (i, k)),
                  pl.BlockSpec((tk, D), lambda i, k: (k, 0)),
                  pl.BlockSpec(memory_space=pl.ANY),
                  vec, vec],
        out_specs=[row, row] if emit_bf16 else [row],
        out_shape=[jax.ShapeDtypeStruct((M, D), F32)] + [jax.ShapeDtypeStruct((M, D), BF16)] * emit_bf16,
        scratch_shapes=[pltpu.VMEM((tm, D), F32), pltpu.SemaphoreType.DMA(())],
        compiler_params=_params(2),
        name="matmul_residual_ln",
    )(x, w, res, g.reshape(1, D), b.reshape(1, D))


def _pool_kernel(u_ref, halo_ref, x_ref, wg_ref, sc_ref, g_ref, b_ref, o_ref, obf_ref, ext_ref, *, tm, C):
    i = pl.program_id(0)
    pos = i * tm + 1 + lax.broadcasted_iota(jnp.int32, (tm, 1), 0)
    for gi, w in enumerate(POOL_WINDOWS):
        cs = slice(gi * C, (gi + 1) * C)
        ext_ref[0:POOL_HALO, :] = jnp.where(i > 0, halo_ref[:, cs], 0.0)
        ext_ref[POOL_HALO:, :] = u_ref[:, cs]
        u = u_ref[:, cs]
        win = u
        for k in range(1, w):
            win = win + ext_ref[pl.ds(POOL_HALO - k, tm), :]
        count = jnp.minimum(pos, w).astype(F32)
        d = win / count - u
        y = jnp.dot(d.astype(BF16), wg_ref[gi], preferred_element_type=F32)
        o_ref[:, cs] = ALPHA * x_ref[:, cs] + y * sc_ref[:, cs]
    for r in range(0, tm, EPI_ROWS):
        rs = slice(r, r + EPI_ROWS)
        y = _layer_norm(o_ref[rs, :], g_ref[...], b_ref[...])
        o_ref[rs, :] = y
        obf_ref[rs, :] = y.astype(BF16)


def pool_mix_ln(u, x, w_group, scale, g, b, *, tm=256):
    S, D = u.shape
    G, C, _ = w_group.shape
    assert G == len(POOL_WINDOWS) and G * C == D and S % tm == 0 and tm % POOL_HALO == 0
    hb = tm // POOL_HALO
    row = pl.BlockSpec((tm, D), lambda i: (i, 0))
    vec = pl.BlockSpec((1, D), lambda i: (0, 0))
    return pl.pallas_call(
        functools.partial(_pool_kernel, tm=tm, C=C),
        grid=(S // tm,),
        in_specs=[row,
                  pl.BlockSpec((POOL_HALO, D), lambda i: (jnp.maximum(i * hb - 1, 0), 0)),
                  row,
                  pl.BlockSpec((G, C, C), lambda i: (0, 0, 0)),
                  vec, vec, vec],
        out_specs=[row, row],
        out_shape=[jax.ShapeDtypeStruct((S, D), F32), jax.ShapeDtypeStruct((S, D), BF16)],
        scratch_shapes=[pltpu.VMEM((tm + POOL_HALO, C), F32)],
        compiler_params=_params(1),
        name="pool_mix_ln",
    )(u, u, x, w_group, scale.reshape(1, D), g.reshape(1, D), b.reshape(1, D))


def _glu_math(x, wg, wu):
    gate = jnp.dot(x, wg, preferred_element_type=F32)
    up = jnp.dot(x, wu, preferred_element_type=F32)
    return (gate * jax.nn.sigmoid(gate) * up).astype(BF16)


def _sb_tile(q, ks, vs, tri2, drop_run, mask):
    tb = tri2.shape[1]
    z = lax.dot_general(q, ks, (((1,), (1,)), ((), ())), preferred_element_type=F32)
    if mask is not None:
        z = jnp.where(mask, z, MASKED_LOGIT)
    sp = jnp.log(1.0 + jnp.exp(-jnp.abs(z)))
    log_beta = jnp.minimum(z, 0.0) - sp
    drop = jnp.maximum(z, 0.0) + sp
    later = [None] * (z.shape[1] // tb)
    for blk in reversed(range(len(later))):
        d = drop[:, blk * tb:(blk + 1) * tb]
        hi = d.astype(BF16)
        lo = (d - hi.astype(F32)).astype(BF16)
        later[blk] = jnp.dot(jnp.concatenate([hi, lo], axis=1), tri2, preferred_element_type=F32) + drop_run
        drop_run = drop_run + jnp.sum(d, axis=1, keepdims=True)
    w = jnp.exp(log_beta - jnp.concatenate(later, axis=1))
    pv = jnp.dot(w.astype(BF16), vs, preferred_element_type=F32)
    return pv, drop_run


def _attn_kernel(q_ref, k_ref, v_ref, tri_ref, o_ref, *, tq, heads):
    i = pl.program_id(1)
    t0 = i * tq
    w0 = pl.multiple_of(jnp.maximum(t0 - tq, 0), tq)
    t_pos = t0 + lax.broadcasted_iota(jnp.int32, (tq, 1), 0)
    mask = (w0 + lax.broadcasted_iota(jnp.int32, (1, 2 * tq), 1)) < t_pos
    zero = jnp.zeros((tq, 1), F32)
    accs, runs = [], []
    for g in range(heads):
        hs = slice(g * HEAD_DIM, (g + 1) * HEAD_DIM)
        pv, run = _sb_tile(q_ref[:, hs], k_ref[pl.ds(w0, 2 * tq), hs], v_ref[pl.ds(w0, 2 * tq), hs],
                           tri_ref[...], zero, mask)
        accs.append(pv)
        runs.append(run)

    def alive(runs):
        return functools.reduce(jnp.minimum, [jnp.min(r) for r in runs]) < -EXP_UNDERFLOW

    def cond(carry):
        j, runs, _ = carry
        return jnp.logical_and(j >= 0, alive(runs))

    def body(carry):
        j, runs, accs = carry
        start = pl.multiple_of(j * tq, tq)
        new_runs, new_accs = [], []
        for g in range(heads):
            hs = slice(g * HEAD_DIM, (g + 1) * HEAD_DIM)
            pv, run = _sb_tile(q_ref[:, hs], k_ref[pl.ds(start, tq), hs], v_ref[pl.ds(start, tq), hs],
                               tri_ref[...], runs[g], None)
            new_accs.append(accs[g] + pv)
            new_runs.append(run)
        return j - 1, tuple(new_runs), tuple(new_accs)

    _, _, accs = lax.while_loop(cond, body, (w0 // tq - 1, tuple(runs), tuple(accs)))
    for g in range(heads):
        o_ref[:, g * HEAD_DIM:(g + 1) * HEAD_DIM] = accs[g].astype(o_ref.dtype)


def _stacked_tri(n):
    r = lax.broadcasted_iota(jnp.int32, (n, n), 0)
    c = lax.broadcasted_iota(jnp.int32, (n, n), 1)
    tri = (r > c).astype(BF16)
    return jnp.concatenate([tri, tri], axis=0)


def stick_breaking_attention(qkv, *, tq=256, heads=4):
    S, D3 = qkv.shape
    D = D3 // 3
    H = D // HEAD_DIM
    assert S % tq == 0 and S >= 2 * tq and H % heads == 0
    hw = heads * HEAD_DIM
    nb = D // hw
    return pl.pallas_call(
        functools.partial(_attn_kernel, tq=tq, heads=heads),
        grid=(nb, S // tq),
        in_specs=[pl.BlockSpec((tq, hw), lambda h, i: (i, h)),
                  pl.BlockSpec((S, hw), lambda h, i: (0, nb + h)),
                  pl.BlockSpec((S, hw), lambda h, i: (0, 2 * nb + h)),
                  pl.BlockSpec((2 * tq, tq), lambda h, i: (0, 0))],
        out_specs=pl.BlockSpec((tq, hw), lambda h, i: (i, h)),
        out_shape=jax.ShapeDtypeStruct((S, D), BF16),
        compiler_params=_params(2),
        name="stick_breaking_attention",
    )(qkv, qkv, qkv, _stacked_tri(tq))


def _router_kernel(h_ref, w_ref, o_ref, *, n_experts):
    h = h_ref[...]
    w = w_ref[...]
    h1 = h.astype(BF16)
    h2 = (h - h1.astype(F32)).astype(BF16)
    h3 = (h - h1.astype(F32) - h2.astype(F32)).astype(BF16)
    w1 = w.astype(BF16)
    w2 = (w - w1.astype(F32)).astype(BF16)
    w3 = (w - w1.astype(F32) - w2.astype(F32)).astype(BF16)
    dot = functools.partial(jnp.dot, preferred_element_type=F32)
    logits = (dot(h3, w1) + dot(h2, w2) + dot(h1, w3)) + (dot(h2, w1) + dot(h1, w2)) + dot(h1, w1)
    lane = lax.broadcasted_iota(jnp.int32, logits.shape, 1)
    lane_f = lane.astype(F32)
    neg = jnp.float32(-jnp.inf)
    logits = jnp.where(lane < n_experts, logits, neg)
    m1 = jnp.max(logits, axis=1, keepdims=True)
    i1 = jnp.min(jnp.where(logits == m1, lane_f, float(LANES)), axis=1, keepdims=True)
    rest = jnp.where(lane_f == i1, neg, logits)
    m2 = jnp.max(rest, axis=1, keepdims=True)
    i2 = jnp.min(jnp.where(rest == m2, lane_f, float(LANES)), axis=1, keepdims=True)
    e2 = jnp.exp(m2 - m1)
    g1 = 1.0 / (1.0 + e2)
    g2 = e2 / (1.0 + e2)
    out = jnp.where(lane == 0, i1, 0.0)
    out = jnp.where(lane == 1, i2, out)
    out = jnp.where(lane == 2, g1, out)
    out = jnp.where(lane == 3, g2, out)
    o_ref[...] = out


def route_top2(h, w_router, *, tm=256):
    S, D = h.shape
    E = w_router.shape[1]
    assert E <= LANES and S % tm == 0
    w_pad = jnp.pad(w_router, ((0, 0), (0, LANES - E)))
    info = pl.pallas_call(
        functools.partial(_router_kernel, n_experts=E),
        grid=(S // tm,),
        in_specs=[pl.BlockSpec((tm, D), lambda i: (i, 0)),
                  pl.BlockSpec((D, LANES), lambda i: (0, 0))],
        out_specs=pl.BlockSpec((tm, LANES), lambda i: (i, 0)),
        out_shape=jax.ShapeDtypeStruct((S, LANES), F32),
        compiler_params=_params(1),
        name="route_top2",
    )(h, w_pad)
    return info[:, 0:TOP_K].astype(jnp.int32), info[:, TOP_K:2 * TOP_K]


def _gather_kernel(src_ref, live_ref, h_ref, o_ref, stage_ref, sems, *, tg):
    i = pl.program_id(0)
    n_live = live_ref[0] // tg
    slot = i % 2

    def row_copy(src_row, r, slot):
        return pltpu.make_async_copy(h_ref.at[pl.ds(src_row, 1), :],
                                     stage_ref.at[slot, pl.ds(r, 1), :], sems.at[slot])

    def issue(step, slot):
        def start_row(r, c):
            row_copy(src_ref[step * tg + r], r, slot).start()
            return c
        lax.fori_loop(0, tg, start_row, 0, unroll=DMA_UNROLL)

    @pl.when(jnp.logical_and(i == 0, n_live > 0))
    def _():
        issue(0, 0)

    @pl.when(i + 1 < n_live)
    def _():
        issue(i + 1, 1 - slot)

    @pl.when(i < n_live)
    def _():
        def wait_row(r, c):
            row_copy(0, r, slot).wait()
            return c
        lax.fori_loop(0, tg, wait_row, 0, unroll=DMA_UNROLL)
        o_ref[...] = stage_ref[slot].astype(BF16)

    @pl.when(i >= n_live)
    def _():
        o_ref[...] = jnp.zeros_like(o_ref)


def gather_rows_bf16(h, src_rows, live_rows, *, tg=256):
    S, D = h.shape
    N = src_rows.shape[0]
    assert N % tg == 0
    return pl.pallas_call(
        functools.partial(_gather_kernel, tg=tg),
        grid_spec=pltpu.PrefetchScalarGridSpec(
            num_scalar_prefetch=2,
            grid=(N // tg,),
            in_specs=[pl.BlockSpec(memory_space=pl.ANY)],
            out_specs=pl.BlockSpec((tg, D), lambda i, src, live: (i, 0)),
            scratch_shapes=[pltpu.VMEM((2, tg, D), F32), pltpu.SemaphoreType.DMA((2,))]),
        out_shape=jax.ShapeDtypeStruct((N, D), BF16),
        compiler_params=_params(1),
        name="gather_rows_bf16",
    )(src_rows, live_rows, h)


def _run_schedule(tile_expert, used):
    T = tile_expert.shape[0]
    t = jnp.arange(T, dtype=jnp.int32)
    prev = jnp.concatenate([tile_expert[:1], tile_expert[:-1]])
    first = jnp.logical_and(jnp.logical_or(t == 0, tile_expert != prev), t < used[0])
    run = jnp.cumsum(first.astype(jnp.int32)) - 1
    later_first = jnp.logical_and(first[None, :], t[None, :] > t[:, None])
    nxt_tile = jnp.min(jnp.where(later_first, t[None, :], T), axis=1)
    nxt = jnp.where(nxt_tile < T, tile_expert[jnp.minimum(nxt_tile, T - 1)], tile_expert[0])
    counts = jnp.stack([used[0], jnp.sum(first.astype(jnp.int32))])
    return first.astype(jnp.int32), run.astype(jnp.int32), nxt.astype(jnp.int32), counts.astype(jnp.int32)


def _stream_run_weights(sched, copies, on_arrival):
    te_ref, first_ref, run_ref, nxt_ref, cnt_ref = sched
    c, t, nc = pl.program_id(0), pl.program_id(1), pl.num_programs(0)
    n_runs = cnt_ref[1]
    seq = c * n_runs + run_ref[t]
    slot = seq % 2

    @pl.when(first_ref[t] == 1)
    def _():
        @pl.when(seq == 0)
        def _():
            for cp in copies(te_ref[t], c, slot):
                cp.start()

        for cp in copies(te_ref[t], c, slot):
            cp.wait()
        next_c = c + (run_ref[t] == n_runs - 1).astype(jnp.int32)

        @pl.when(next_c < nc)
        def _():
            for cp in copies(nxt_ref[t], next_c, 1 - slot):
                cp.start()

        on_arrival(slot)


def _grouped_glu_kernel(te_ref, first_ref, run_ref, nxt_ref, cnt_ref, x_ref, wg_hbm, wu_hbm, o_ref,
                        stage_ref, wgb_ref, wub_ref, sems):
    t = pl.program_id(1)
    tf = o_ref.shape[1]

    def copies(e, f, slot):
        cols = pl.ds(pl.multiple_of(f * tf, tf), tf)
        return [pltpu.make_async_copy(w_hbm.at[e, :, cols], stage_ref.at[slot, k], sems.at[slot, k])
                for k, w_hbm in enumerate((wg_hbm, wu_hbm))]

    def cast(slot):
        wgb_ref[...] = stage_ref[slot, 0].astype(BF16)
        wub_ref[...] = stage_ref[slot, 1].astype(BF16)

    _stream_run_weights((te_ref, first_ref, run_ref, nxt_ref, cnt_ref), copies, cast)

    @pl.when(t < cnt_ref[0])
    def _():
        o_ref[...] = _glu_math(x_ref[...], wgb_ref[...], wub_ref[...])

    @pl.when(t >= cnt_ref[0])
    def _():
        o_ref[...] = jnp.zeros_like(o_ref)


def grouped_glu(x, wg, wu, tile_expert, used, *, tm, tf):
    N, K = x.shape
    E, _, F = wg.shape
    assert N % tm == 0 and F % tf == 0
    sched = (tile_expert,) + _run_schedule(tile_expert, used)
    return pl.pallas_call(
        _grouped_glu_kernel,
        grid_spec=pltpu.PrefetchScalarGridSpec(
            num_scalar_prefetch=len(sched),
            grid=(F // tf, N // tm),
            in_specs=[pl.BlockSpec((tm, K), lambda f, t, *_: (t, 0)),
                      pl.BlockSpec(memory_space=pl.ANY), pl.BlockSpec(memory_space=pl.ANY)],
            out_specs=pl.BlockSpec((tm, tf), lambda f, t, *_: (t, f)),
            scratch_shapes=[pltpu.VMEM((2, 2, K, tf), F32), pltpu.VMEM((K, tf), BF16),
                            pltpu.VMEM((K, tf), BF16), pltpu.SemaphoreType.DMA((2, 2))]),
        out_shape=jax.ShapeDtypeStruct((N, F), BF16),
        compiler_params=_params(2),
        name="grouped_glu",
    )(*sched, x, wg, wu)


def _expert_down_kernel(te_ref, first_ref, run_ref, nxt_ref, cnt_ref, x_ref, w_hbm, o_ref,
                        stage_ref, wb_ref, sems):
    t = pl.program_id(1)
    tn = o_ref.shape[1]

    def copies(e, n, slot):
        cols = pl.ds(pl.multiple_of(n * tn, tn), tn)
        return [pltpu.make_async_copy(w_hbm.at[e, :, cols], stage_ref.at[slot], sems.at[slot])]

    def cast(slot):
        wb_ref[...] = stage_ref[slot].astype(BF16)

    _stream_run_weights((te_ref, first_ref, run_ref, nxt_ref, cnt_ref), copies, cast)

    @pl.when(t < cnt_ref[0])
    def _():
        o_ref[...] = jnp.dot(x_ref[...], wb_ref[...], preferred_element_type=F32)

    @pl.when(t >= cnt_ref[0])
    def _():
        o_ref[...] = jnp.zeros_like(o_ref)


def expert_down(hid, wd, tile_expert, used, *, tm, tn=512):
    N, F = hid.shape
    E, _, D = wd.shape
    assert N % tm == 0 and D % tn == 0
    sched = (tile_expert,) + _run_schedule(tile_expert, used)
    return pl.pallas_call(
        _expert_down_kernel,
        grid_spec=pltpu.PrefetchScalarGridSpec(
            num_scalar_prefetch=len(sched),
            grid=(D // tn, N // tm),
            in_specs=[pl.BlockSpec((tm, F), lambda n, t, *_: (t, 0)),
                      pl.BlockSpec(memory_space=pl.ANY)],
            out_specs=pl.BlockSpec((tm, tn), lambda n, t, *_: (t, n)),
            scratch_shapes=[pltpu.VMEM((2, F, tn), F32), pltpu.VMEM((F, tn), BF16),
                            pltpu.SemaphoreType.DMA((2,))]),
        out_shape=jax.ShapeDtypeStruct((N, D), F32),
        compiler_params=_params(2),
        name="expert_down",
    )(*sched, hid, wd)


def _combine_kernel(pos_ref, y_ref, h_ref, gate_ref, g_ref, b_ref, o_ref, stage_ref, sems, *, tc):
    i = pl.program_id(0)
    n = pl.num_programs(0)
    slot = i % 2

    def row_copy(y_row, r, k, slot):
        return pltpu.make_async_copy(y_ref.at[pl.ds(y_row, 1), :],
                                     stage_ref.at[slot, k, pl.ds(r, 1), :], sems.at[slot])

    def issue(step, slot):
        def start_row(r, c):
            for k in range(TOP_K):
                row_copy(pos_ref[(step * tc + r) * TOP_K + k], r, k, slot).start()
            return c
        lax.fori_loop(0, tc, start_row, 0, unroll=DMA_UNROLL)

    @pl.when(i == 0)
    def _():
        issue(0, 0)

    @pl.when(i + 1 < n)
    def _():
        issue(i + 1, 1 - slot)

    def wait_row(r, c):
        for k in range(TOP_K):
            row_copy(0, r, k, slot).wait()
        return c
    lax.fori_loop(0, tc, wait_row, 0, unroll=DMA_UNROLL)

    for r in range(0, tc, EPI_ROWS):
        rs = slice(r, r + EPI_ROWS)
        gates = gate_ref[rs, :]
        moe = gates[:, 0:1] * stage_ref[slot, 0, rs, :]
        for k in range(1, TOP_K):
            moe = moe + gates[:, k:k + 1] * stage_ref[slot, k, rs, :]
        o_ref[rs, :] = _layer_norm(ALPHA * h_ref[rs, :] + moe, g_ref[...], b_ref[...])


def combine_ln(y, pos, h, gates, g, b, *, tc=256):
    S, D = h.shape
    assert S % tc == 0
    row = pl.BlockSpec((tc, D), lambda i, pos: (i, 0))
    vec = pl.BlockSpec((1, D), lambda i, pos: (0, 0))
    return pl.pallas_call(
        functools.partial(_combine_kernel, tc=tc),
        grid_spec=pltpu.PrefetchScalarGridSpec(
            num_scalar_prefetch=1,
            grid=(S // tc,),
            in_specs=[pl.BlockSpec(memory_space=pl.ANY), row,
                      pl.BlockSpec((tc, TOP_K), lambda i, pos: (i, 0)), vec, vec],
            out_specs=row,
            scratch_shapes=[pltpu.VMEM((2, TOP_K, tc, D), F32), pltpu.SemaphoreType.DMA((2,))]),
        out_shape=jax.ShapeDtypeStruct((S, D), F32),
        compiler_params=_params(1),
        name="combine_ln",
    )(pos.reshape(-1), y, h, gates, g.reshape(1, D), b.reshape(1, D))


def _routing_tables(idx, n_experts, tm):
    S, K = idx.shape
    flat = idx.reshape(-1)
    onehot = (flat[:, None] == jnp.arange(n_experts, dtype=jnp.int32)[None, :]).astype(jnp.int32)
    csum = jnp.cumsum(onehot, axis=0)
    rank = jnp.sum(onehot * csum, axis=1) - 1
    counts = csum[-1]
    padded = (counts + tm - 1) // tm * tm
    ends = jnp.cumsum(padded)
    starts = ends - padded
    pos = jnp.sum(onehot * starts[None, :], axis=1) + rank
    n_rows = S * K + n_experts * tm
    src = jnp.zeros((n_rows,), jnp.int32).at[pos].set(jnp.arange(S * K, dtype=jnp.int32) // K)
    used = ends[-1] // tm
    tile_start = jnp.minimum(jnp.arange(n_rows // tm, dtype=jnp.int32), used - 1) * tm
    tile_expert = jnp.sum((tile_start[:, None] >= ends[None, :]).astype(jnp.int32), axis=1)
    return pos.reshape(S, K), src, tile_expert.astype(jnp.int32), used.reshape(1).astype(jnp.int32)


def moe_ln(h, w_router, wg, wu, wd, g, b, *, tm=256):
    E = w_router.shape[1]
    idx, gates = route_top2(h, w_router)
    pos, src, tile_expert, used = _routing_tables(idx, E, tm)
    xg = gather_rows_bf16(h, src, used * tm)
    hid = grouped_glu(xg, wg, wu, tile_expert, used, tm=tm, tf=EXPERT_FF_TILE)
    y = expert_down(hid, wd, tile_expert, used, tm=tm)
    return combine_ln(y, pos, h, gates, g, b)


def kernel(x, l0_pool_w_in, l0_pool_w_group, l0_pool_scale, l0_ln1_g, l0_ln1_b, l0_ffn_w_gate, l0_ffn_w_up, l0_ffn_w_down, l0_ln2_g, l0_ln2_b, l1_attn_w_qkv, l1_attn_w_o, l1_ln1_g, l1_ln1_b, l1_moe_w_router, l1_moe_w_gate, l1_moe_w_up, l1_moe_w_down, l1_ln2_g, l1_ln2_b):
    B, S, D = x.shape
    x0 = x.reshape(B * S, D)
    assert B == 1, "sequence mixing below treats all rows as one sequence"
    bf = lambda w: w.astype(BF16)

    u = matmul(bf(x0), l0_pool_w_in, F32)
    x1, x1b = pool_mix_ln(u, x0, bf(l0_pool_w_group), l0_pool_scale, l0_ln1_g, l0_ln1_b)
    n_tiles = (B * S) // DENSE_ROWS
    hid = grouped_glu(x1b, l0_ffn_w_gate[None], l0_ffn_w_up[None], jnp.zeros((n_tiles,), jnp.int32),
                      jnp.full((1,), n_tiles, jnp.int32), tm=DENSE_ROWS, tf=DENSE_FF_TILE)
    x2, x2b = matmul_residual_ln(hid, bf(l0_ffn_w_down), x1, l0_ln2_g, l0_ln2_b, emit_bf16=True)

    qkv = matmul(x2b, l1_attn_w_qkv, BF16, scaled_cols=D, scale=HEAD_DIM ** -0.5)
    o = stick_breaking_attention(qkv)
    (x3,) = matmul_residual_ln(o, l1_attn_w_o, x2, l1_ln1_g, l1_ln1_b, emit_bf16=False)
    x4 = moe_ln(x3, l1_moe_w_router, l1_moe_w_gate, l1_moe_w_up, l1_moe_w_down, l1_ln2_g, l1_ln2_b)
    return x4.reshape(B, S, D)
```

```python
import functools

import jax
import jax.numpy as jnp
from jax import lax
from jax.experimental import pallas as pl
from jax.experimental.pallas import tpu as pltpu

HEAD_DIM = 128
POOL_WINDOWS = (2, 4, 8, 16)
POOL_HALO = 16
TOP_K = 2
LN_EPS = 1e-5
DEPTH = 2
ALPHA = (2 * DEPTH) ** 0.25
LANES = 128
DENSE_ROWS = 1024
DENSE_FF_TILE = 256
DOWN_ROWS = 512
DOWN_OUT_TILE = 256
EXPERT_ROWS = 256
EXPERT_FF_TILE = 512
EXPERT_OUT_TILE = 1024
DMA_UNROLL = 8
EPI_ROWS = 128
VMEM_LIMIT = 56 * 1024 * 1024
EXP_UNDERFLOW = -104.0
MASKED_LOGIT = -1e30

BF16 = jnp.bfloat16
F32 = jnp.float32


def _params(n_axes):
    return pltpu.CompilerParams(dimension_semantics=("arbitrary",) * n_axes,
                                vmem_limit_bytes=VMEM_LIMIT)


def _layer_norm(t, g, b):
    mu = jnp.mean(t, axis=-1, keepdims=True)
    c = t - mu
    var = jnp.mean(c * c, axis=-1, keepdims=True)
    return c * lax.rsqrt(var + LN_EPS) * g + b


def _scaled_dot(x, w, *, out_dtype, scaled_blocks, scale):
    acc = jnp.dot(x, w, preferred_element_type=F32)
    if scaled_blocks:
        acc = acc * jnp.where(pl.program_id(0) < scaled_blocks, scale, 1.0)
    return acc.astype(out_dtype)


def matmul(x, w, out_dtype, *, ts=1024, tc=512, scaled_cols=0, scale=1.0):
    M, K = x.shape
    ts = min(ts, M)
    assert M % ts == 0 and scaled_cols % tc == 0
    math = functools.partial(_scaled_dot, out_dtype=out_dtype, scaled_blocks=scaled_cols // tc, scale=scale)
    return expert_stream_matmul(x, (w[None],), jnp.zeros((1,), jnp.int32), jnp.full((1,), M // ts, jnp.int32),
                                math, out_dtype, ts=ts, tc=tc)


def _residual_ln_kernel(y_ref, res_ref, g_ref, b_ref, o_ref, *obf_ref):
    for r in range(0, o_ref.shape[0], EPI_ROWS):
        rs = slice(r, r + EPI_ROWS)
        y = _layer_norm(ALPHA * res_ref[rs, :] + y_ref[rs, :], g_ref[...], b_ref[...])
        o_ref[rs, :] = y
        for ref in obf_ref:
            ref[rs, :] = y.astype(BF16)


def residual_ln(y, res, g, b, *, emit_bf16, tr=256):
    M, D = y.shape
    tr = min(tr, M)
    assert M % tr == 0
    row = pl.BlockSpec((tr, D), lambda i: (i, 0))
    vec = pl.BlockSpec((1, D), lambda i: (0, 0))
    return pl.pallas_call(
        _residual_ln_kernel,
        grid=(M // tr,),
        in_specs=[row, row, vec, vec],
        out_specs=[row, row] if emit_bf16 else [row],
        out_shape=[jax.ShapeDtypeStruct((M, D), F32)] + [jax.ShapeDtypeStruct((M, D), BF16)] * emit_bf16,
        compiler_params=_params(1),
        name="residual_ln",
    )(y, res, g.reshape(1, D), b.reshape(1, D))


def _pool_kernel(u_ref, halo_ref, x_ref, wg_ref, sc_ref, g_ref, b_ref, o_ref, obf_ref, ext_ref, *, tm, C):
    i = pl.program_id(0)
    pos = i * tm + 1 + lax.broadcasted_iota(jnp.int32, (tm, 1), 0)
    for gi, w in enumerate(POOL_WINDOWS):
        cs = slice(gi * C, (gi + 1) * C)
        ext_ref[0:POOL_HALO, :] = jnp.where(i > 0, halo_ref[:, cs], 0.0)
        ext_ref[POOL_HALO:, :] = u_ref[:, cs]
        u = u_ref[:, cs]
        win = u
        for k in range(1, w):
            win = win + ext_ref[pl.ds(POOL_HALO - k, tm), :]
        count = jnp.minimum(pos, w).astype(F32)
        d = win / count - u
        y = jnp.dot(d.astype(BF16), wg_ref[gi], preferred_element_type=F32)
        o_ref[:, cs] = ALPHA * x_ref[:, cs] + y * sc_ref[:, cs]
    for r in range(0, tm, EPI_ROWS):
        rs = slice(r, r + EPI_ROWS)
        y = _layer_norm(o_ref[rs, :], g_ref[...], b_ref[...])
        o_ref[rs, :] = y
        obf_ref[rs, :] = y.astype(BF16)


def pool_mix_ln(u, x, w_group, scale, g, b, *, tm=256):
    S, D = u.shape
    G, C, _ = w_group.shape
    assert G == len(POOL_WINDOWS) and G * C == D and S % tm == 0 and tm % POOL_HALO == 0
    hb = tm // POOL_HALO
    row = pl.BlockSpec((tm, D), lambda i: (i, 0))
    vec = pl.BlockSpec((1, D), lambda i: (0, 0))
    return pl.pallas_call(
        functools.partial(_pool_kernel, tm=tm, C=C),
        grid=(S // tm,),
        in_specs=[row,
                  pl.BlockSpec((POOL_HALO, D), lambda i: (jnp.maximum(i * hb - 1, 0), 0)),
                  row,
                  pl.BlockSpec((G, C, C), lambda i: (0, 0, 0)),
                  vec, vec, vec],
        out_specs=[row, row],
        out_shape=[jax.ShapeDtypeStruct((S, D), F32), jax.ShapeDtypeStruct((S, D), BF16)],
        scratch_shapes=[pltpu.VMEM((tm + POOL_HALO, C), F32)],
        compiler_params=_params(1),
        name="pool_mix_ln",
    )(u, u, x, w_group, scale.reshape(1, D), g.reshape(1, D), b.reshape(1, D))


def _glu_math(x, wg, wu):
    gate = jnp.dot(x, wg, preferred_element_type=F32)
    up = jnp.dot(x, wu, preferred_element_type=F32)
    return (gate * jax.nn.sigmoid(gate) * up).astype(BF16)


def _sb_tile(q, ks, vs, tri2, drop_run, mask):
    tb = tri2.shape[1]
    z = lax.dot_general(q, ks, (((1,), (1,)), ((), ())), preferred_element_type=F32)
    if mask is not None:
        z = jnp.where(mask, z, MASKED_LOGIT)
    sp = jnp.log(1.0 + jnp.exp(-jnp.abs(z)))
    log_beta = jnp.minimum(z, 0.0) - sp
    drop = jnp.maximum(z, 0.0) + sp
    later = [None] * (z.shape[1] // tb)
    for blk in reversed(range(len(later))):
        d = drop[:, blk * tb:(blk + 1) * tb]
        hi = d.astype(BF16)
        lo = (d - hi.astype(F32)).astype(BF16)
        later[blk] = jnp.dot(jnp.concatenate([hi, lo], axis=1), tri2, preferred_element_type=F32) + drop_run
        drop_run = drop_run + jnp.sum(d, axis=1, keepdims=True)
    w = jnp.exp(log_beta - jnp.concatenate(later, axis=1))
    pv = jnp.dot(w.astype(BF16), vs, preferred_element_type=F32)
    return pv, drop_run


def _attn_kernel(q_ref, k_ref, v_ref, tri_ref, o_ref, *, tq, heads):
    i = pl.program_id(1)
    t0 = i * tq
    w0 = pl.multiple_of(jnp.maximum(t0 - tq, 0), tq)
    t_pos = t0 + lax.broadcasted_iota(jnp.int32, (tq, 1), 0)
    mask = (w0 + lax.broadcasted_iota(jnp.int32, (1, 2 * tq), 1)) < t_pos
    zero = jnp.zeros((tq, 1), F32)
    accs, runs = [], []
    for g in range(heads):
        hs = slice(g * HEAD_DIM, (g + 1) * HEAD_DIM)
        pv, run = _sb_tile(q_ref[:, hs], k_ref[pl.ds(w0, 2 * tq), hs], v_ref[pl.ds(w0, 2 * tq), hs],
                           tri_ref[...], zero, mask)
        accs.append(pv)
        runs.append(run)

    def alive(runs):
        return functools.reduce(jnp.minimum, [jnp.min(r) for r in runs]) < -EXP_UNDERFLOW

    def cond(carry):
        j, runs, _ = carry
        return jnp.logical_and(j >= 0, alive(runs))

    def body(carry):
        j, runs, accs = carry
        start = pl.multiple_of(j * tq, tq)
        new_runs, new_accs = [], []
        for g in range(heads):
            hs = slice(g * HEAD_DIM, (g + 1) * HEAD_DIM)
            pv, run = _sb_tile(q_ref[:, hs], k_ref[pl.ds(start, tq), hs], v_ref[pl.ds(start, tq), hs],
                               tri_ref[...], runs[g], None)
            new_accs.append(accs[g] + pv)
            new_runs.append(run)
        return j - 1, tuple(new_runs), tuple(new_accs)

    _, _, accs = lax.while_loop(cond, body, (w0 // tq - 1, tuple(runs), tuple(accs)))
    for g in range(heads):
        o_ref[:, g * HEAD_DIM:(g + 1) * HEAD_DIM] = accs[g].astype(o_ref.dtype)


def _stacked_tri(n):
    r = lax.broadcasted_iota(jnp.int32, (n, n), 0)
    c = lax.broadcasted_iota(jnp.int32, (n, n), 1)
    tri = (r > c).astype(BF16)
    return jnp.concatenate([tri, tri], axis=0)


def stick_breaking_attention(qkv, *, tq=256, heads=4):
    S, D3 = qkv.shape
    D = D3 // 3
    H = D // HEAD_DIM
    assert S % tq == 0 and S >= 2 * tq and H % heads == 0
    hw = heads * HEAD_DIM
    nb = D // hw
    return pl.pallas_call(
        functools.partial(_attn_kernel, tq=tq, heads=heads),
        grid=(nb, S // tq),
        in_specs=[pl.BlockSpec((tq, hw), lambda h, i: (i, h)),
                  pl.BlockSpec((S, hw), lambda h, i: (0, nb + h)),
                  pl.BlockSpec((S, hw), lambda h, i: (0, 2 * nb + h)),
                  pl.BlockSpec((2 * tq, tq), lambda h, i: (0, 0))],
        out_specs=pl.BlockSpec((tq, hw), lambda h, i: (i, h)),
        out_shape=jax.ShapeDtypeStruct((S, D), BF16),
        compiler_params=_params(2),
        name="stick_breaking_attention",
    )(qkv, qkv, qkv, _stacked_tri(tq))


def _router_kernel(h_ref, w_ref, o_ref, *, n_experts):
    h = h_ref[...]
    w = w_ref[...]
    h1 = h.astype(BF16)
    h2 = (h - h1.astype(F32)).astype(BF16)
    h3 = (h - h1.astype(F32) - h2.astype(F32)).astype(BF16)
    w1 = w.astype(BF16)
    w2 = (w - w1.astype(F32)).astype(BF16)
    w3 = (w - w1.astype(F32) - w2.astype(F32)).astype(BF16)
    dot = functools.partial(jnp.dot, preferred_element_type=F32)
    logits = (dot(h3, w1) + dot(h2, w2) + dot(h1, w3)) + (dot(h2, w1) + dot(h1, w2)) + dot(h1, w1)
    lane = lax.broadcasted_iota(jnp.int32, logits.shape, 1)
    lane_f = lane.astype(F32)
    neg = jnp.float32(-jnp.inf)
    logits = jnp.where(lane < n_experts, logits, neg)
    m1 = jnp.max(logits, axis=1, keepdims=True)
    i1 = jnp.min(jnp.where(logits == m1, lane_f, float(LANES)), axis=1, keepdims=True)
    rest = jnp.where(lane_f == i1, neg, logits)
    m2 = jnp.max(rest, axis=1, keepdims=True)
    i2 = jnp.min(jnp.where(rest == m2, lane_f, float(LANES)), axis=1, keepdims=True)
    e2 = jnp.exp(m2 - m1)
    g1 = 1.0 / (1.0 + e2)
    g2 = e2 / (1.0 + e2)
    out = jnp.where(lane == 0, i1, 0.0)
    out = jnp.where(lane == 1, i2, out)
    out = jnp.where(lane == 2, g1, out)
    out = jnp.where(lane == 3, g2, out)
    o_ref[...] = out


def route_top2(h, w_router, *, tm=256):
    S, D = h.shape
    E = w_router.shape[1]
    assert E <= LANES and S % tm == 0
    w_pad = jnp.pad(w_router, ((0, 0), (0, LANES - E)))
    info = pl.pallas_call(
        functools.partial(_router_kernel, n_experts=E),
        grid=(S // tm,),
        in_specs=[pl.BlockSpec((tm, D), lambda i: (i, 0)),
                  pl.BlockSpec((D, LANES), lambda i: (0, 0))],
        out_specs=pl.BlockSpec((tm, LANES), lambda i: (i, 0)),
        out_shape=jax.ShapeDtypeStruct((S, LANES), F32),
        compiler_params=_params(1),
        name="route_top2",
    )(h, w_pad)
    return info[:, 0:TOP_K].astype(jnp.int32), info[:, TOP_K:2 * TOP_K]


def _gather_kernel(src_ref, live_ref, h_ref, o_ref, stage_ref, sems, *, tg):
    i = pl.program_id(0)
    n_live = live_ref[0] // tg
    slot = i % 2

    def row_copy(src_row, r, slot):
        return pltpu.make_async_copy(h_ref.at[pl.ds(src_row, 1), :],
                                     stage_ref.at[slot, pl.ds(r, 1), :], sems.at[slot])

    def issue(step, slot):
        def start_row(r, c):
            row_copy(src_ref[step * tg + r], r, slot).start()
            return c
        lax.fori_loop(0, tg, start_row, 0, unroll=DMA_UNROLL)

    @pl.when(jnp.logical_and(i == 0, n_live > 0))
    def _():
        issue(0, 0)

    @pl.when(i + 1 < n_live)
    def _():
        issue(i + 1, 1 - slot)

    @pl.when(i < n_live)
    def _():
        def wait_row(r, c):
            row_copy(0, r, slot).wait()
            return c
        lax.fori_loop(0, tg, wait_row, 0, unroll=DMA_UNROLL)
        o_ref[...] = stage_ref[slot].astype(BF16)

    @pl.when(i >= n_live)
    def _():
        o_ref[...] = jnp.zeros_like(o_ref)


def gather_rows_bf16(h, src_rows, live_rows, *, tg=256):
    S, D = h.shape
    N = src_rows.shape[0]
    assert N % tg == 0
    return pl.pallas_call(
        functools.partial(_gather_kernel, tg=tg),
        grid_spec=pltpu.PrefetchScalarGridSpec(
            num_scalar_prefetch=2,
            grid=(N // tg,),
            in_specs=[pl.BlockSpec(memory_space=pl.ANY)],
            out_specs=pl.BlockSpec((tg, D), lambda i, src, live: (i, 0)),
            scratch_shapes=[pltpu.VMEM((2, tg, D), F32), pltpu.SemaphoreType.DMA((2,))]),
        out_shape=jax.ShapeDtypeStruct((N, D), BF16),
        compiler_params=_params(1),
        name="gather_rows_bf16",
    )(src_rows, live_rows, h)


def _expert_stream_kernel(start_ref, nsub_ref, x_hbm, *rest, n_w, ts, math):
    w_hbm, o_hbm = rest[:n_w], rest[n_w]
    wstage, wb, xbuf, obuf, zbuf, wsem, xsem, osem, zsem = rest[n_w + 1:]
    c, e = pl.program_id(0), pl.program_id(1)
    nc, ne = pl.num_programs(0), pl.num_programs(1)
    tc = obuf.shape[2]
    n = nsub_ref[e]

    def cols_of(c):
        return pl.ds(pl.multiple_of(c * tc, tc), tc)

    def rows_of(e, s):
        return pl.ds(pl.multiple_of(start_ref[e] + s * ts, ts), ts)

    def w_copies(e, c):
        return [pltpu.make_async_copy(w_hbm[k].at[e, :, cols_of(c)], wstage.at[k], wsem.at[k])
                for k in range(n_w)]

    def x_copy(e, s, slot):
        return pltpu.make_async_copy(x_hbm.at[rows_of(e, s), :], xbuf.at[slot], xsem.at[slot])

    def o_copy(s, slot):
        return pltpu.make_async_copy(obuf.at[slot], o_hbm.at[rows_of(e, s), cols_of(c)], osem.at[slot])

    @pl.when(jnp.logical_and(c == 0, e == 0))
    def _():
        for cp in w_copies(e, c):
            cp.start()

        @pl.when(n > 0)
        def _():
            x_copy(e, 0, 0).start()

    for cp in w_copies(e, c):
        cp.wait()
    for k in range(n_w):
        wb[k] = wstage[k].astype(BF16)
    wraps = e == ne - 1
    e_next = jnp.where(wraps, 0, e + 1)
    c_next = c + wraps.astype(jnp.int32)

    @pl.when(c_next < nc)
    def _():
        for cp in w_copies(e_next, c_next):
            cp.start()

    def sub_tile(s, carry):
        slot = s % 2

        @pl.when(s + 1 < n)
        def _():
            x_copy(e, s + 1, 1 - slot).start()

        x_copy(e, s, slot).wait()

        @pl.when(s >= 2)
        def _():
            o_copy(s - 2, slot).wait()

        obuf[slot] = math(xbuf[slot], *[wb[k] for k in range(n_w)])
        o_copy(s, slot).start()
        return carry

    lax.fori_loop(0, n, sub_tile, 0)

    @pl.when(jnp.logical_and(c_next < nc, nsub_ref[e_next] > 0))
    def _():
        x_copy(e_next, 0, 0).start()

    @pl.when(n >= 2)
    def _():
        o_copy(n - 2, n % 2).wait()

    @pl.when(n >= 1)
    def _():
        o_copy(n - 1, (n - 1) % 2).wait()

    @pl.when(e == ne - 1)
    def _():
        end = start_ref[e] + n * ts
        n_dead = (o_hbm.shape[0] - end) // ts
        zbuf[...] = jnp.zeros_like(zbuf)

        def z_copy(d):
            rows = pl.ds(pl.multiple_of(end + d * ts, ts), ts)
            return pltpu.make_async_copy(zbuf, o_hbm.at[rows, cols_of(c)], zsem)

        lax.fori_loop(0, n_dead, lambda d, carry: (z_copy(d).start(), carry)[1], 0)
        lax.fori_loop(0, n_dead, lambda d, carry: (z_copy(d).wait(), carry)[1], 0)


def expert_stream_matmul(x, ws, row_start, n_sub, math, out_dtype, *, ts, tc):
    N, K = x.shape
    E, _, C = ws[0].shape
    assert C % tc == 0 and N % ts == 0
    n_w = len(ws)
    any_spec = pl.BlockSpec(memory_space=pl.ANY)
    return pl.pallas_call(
        functools.partial(_expert_stream_kernel, n_w=n_w, ts=ts, math=math),
        grid_spec=pltpu.PrefetchScalarGridSpec(
            num_scalar_prefetch=2,
            grid=(C // tc, E),
            in_specs=[any_spec] * (1 + n_w),
            out_specs=any_spec,
            scratch_shapes=[pltpu.VMEM((n_w, K, tc), F32), pltpu.VMEM((n_w, K, tc), BF16),
                            pltpu.VMEM((2, ts, K), BF16), pltpu.VMEM((2, ts, tc), out_dtype),
                            pltpu.VMEM((ts, tc), out_dtype),
                            pltpu.SemaphoreType.DMA((n_w,)), pltpu.SemaphoreType.DMA((2,)),
                            pltpu.SemaphoreType.DMA((2,)), pltpu.SemaphoreType.DMA(())]),
        out_shape=jax.ShapeDtypeStruct((N, C), out_dtype),
        compiler_params=_params(2),
        name="expert_stream_matmul",
    )(row_start, n_sub, x, *ws)


def _down_math(x, w):
    return jnp.dot(x, w, preferred_element_type=F32)


def _combine_kernel(pos_ref, y_ref, h_ref, gate_ref, g_ref, b_ref, o_ref, stage_ref, sems, *, tc):
    i = pl.program_id(0)
    n = pl.num_programs(0)
    slot = i % 2

    def row_copy(y_row, r, k, slot):
        return pltpu.make_async_copy(y_ref.at[pl.ds(y_row, 1), :],
                                     stage_ref.at[slot, k, pl.ds(r, 1), :], sems.at[slot])

    def issue(step, slot):
        def start_row(r, c):
            for k in range(TOP_K):
                row_copy(pos_ref[(step * tc + r) * TOP_K + k], r, k, slot).start()
            return c
        lax.fori_loop(0, tc, start_row, 0, unroll=DMA_UNROLL)

    @pl.when(i == 0)
    def _():
        issue(0, 0)

    @pl.when(i + 1 < n)
    def _():
        issue(i + 1, 1 - slot)

    def wait_row(r, c):
        for k in range(TOP_K):
            row_copy(0, r, k, slot).wait()
        return c
    lax.fori_loop(0, tc, wait_row, 0, unroll=DMA_UNROLL)

    for r in range(0, tc, EPI_ROWS):
        rs = slice(r, r + EPI_ROWS)
        gates = gate_ref[rs, :]
        moe = gates[:, 0:1] * stage_ref[slot, 0, rs, :]
        for k in range(1, TOP_K):
            moe = moe + gates[:, k:k + 1] * stage_ref[slot, k, rs, :]
        o_ref[rs, :] = _layer_norm(ALPHA * h_ref[rs, :] + moe, g_ref[...], b_ref[...])


def combine_ln(y, pos, h, gates, g, b, *, tc=256):
    S, D = h.shape
    assert S % tc == 0
    row = pl.BlockSpec((tc, D), lambda i, pos: (i, 0))
    vec = pl.BlockSpec((1, D), lambda i, pos: (0, 0))
    return pl.pallas_call(
        functools.partial(_combine_kernel, tc=tc),
        grid_spec=pltpu.PrefetchScalarGridSpec(
            num_scalar_prefetch=1,
            grid=(S // tc,),
            in_specs=[pl.BlockSpec(memory_space=pl.ANY), row,
                      pl.BlockSpec((tc, TOP_K), lambda i, pos: (i, 0)), vec, vec],
            out_specs=row,
            scratch_shapes=[pltpu.VMEM((2, TOP_K, tc, D), F32), pltpu.SemaphoreType.DMA((2,))]),
        out_shape=jax.ShapeDtypeStruct((S, D), F32),
        compiler_params=_params(1),
        name="combine_ln",
    )(pos.reshape(-1), y, h, gates, g.reshape(1, D), b.reshape(1, D))


def _routing_tables(idx, n_experts, tm):
    S, K = idx.shape
    flat = idx.reshape(-1)
    onehot = (flat[:, None] == jnp.arange(n_experts, dtype=jnp.int32)[None, :]).astype(jnp.int32)
    csum = jnp.cumsum(onehot, axis=0)
    rank = jnp.sum(onehot * csum, axis=1) - 1
    counts = csum[-1]
    padded = (counts + tm - 1) // tm * tm
    ends = jnp.cumsum(padded)
    starts = ends - padded
    pos = jnp.sum(onehot * starts[None, :], axis=1) + rank
    n_rows = S * K + n_experts * tm
    src = jnp.zeros((n_rows,), jnp.int32).at[pos].set(jnp.arange(S * K, dtype=jnp.int32) // K)
    i32 = lambda a: a.astype(jnp.int32)
    return pos.reshape(S, K), src, i32(starts), i32(padded // tm), i32(ends[-1:])


def moe_ln(h, w_router, wg, wu, wd, g, b):
    E = w_router.shape[1]
    idx, gates = route_top2(h, w_router)
    pos, src, starts, n_sub, live = _routing_tables(idx, E, EXPERT_ROWS)
    xg = gather_rows_bf16(h, src, live, tg=EXPERT_ROWS)
    hid = expert_stream_matmul(xg, (wg, wu), starts, n_sub, _glu_math, BF16,
                               ts=EXPERT_ROWS, tc=EXPERT_FF_TILE)
    y = expert_stream_matmul(hid, (wd,), starts, n_sub, _down_math, F32,
                             ts=EXPERT_ROWS, tc=EXPERT_OUT_TILE)
    return combine_ln(y, pos, h, gates, g, b)


def kernel(x, l0_pool_w_in, l0_pool_w_group, l0_pool_scale, l0_ln1_g, l0_ln1_b, l0_ffn_w_gate, l0_ffn_w_up, l0_ffn_w_down, l0_ln2_g, l0_ln2_b, l1_attn_w_qkv, l1_attn_w_o, l1_ln1_g, l1_ln1_b, l1_moe_w_router, l1_moe_w_gate, l1_moe_w_up, l1_moe_w_down, l1_ln2_g, l1_ln2_b):
    B, S, D = x.shape
    x0 = x.reshape(B * S, D)
    assert B == 1, "sequence mixing below treats all rows as one sequence"
    bf = lambda w: w.astype(BF16)

    u = matmul(bf(x0), l0_pool_w_in, F32)
    x1, x1b = pool_mix_ln(u, x0, bf(l0_pool_w_group), l0_pool_scale, l0_ln1_g, l0_ln1_b)
    assert (B * S) % DENSE_ROWS == 0
    hid = expert_stream_matmul(x1b, (l0_ffn_w_gate[None], l0_ffn_w_up[None]), jnp.zeros((1,), jnp.int32),
                               jnp.full((1,), (B * S) // DENSE_ROWS, jnp.int32), _glu_math, BF16,
                               ts=DENSE_ROWS, tc=DENSE_FF_TILE)
    x2, x2b = residual_ln(matmul(hid, l0_ffn_w_down, F32, ts=DOWN_ROWS, tc=DOWN_OUT_TILE), x1,
                          l0_ln2_g, l0_ln2_b, emit_bf16=True)

    qkv = matmul(x2b, l1_attn_w_qkv, BF16, scaled_cols=D, scale=HEAD_DIM ** -0.5)
    o = stick_breaking_attention(qkv)
    (x3,) = residual_ln(matmul(o, l1_attn_w_o, F32), x2, l1_ln1_g, l1_ln1_b, emit_bf16=False)
    x4 = moe_ln(x3, l1_moe_w_router, l1_moe_w_gate, l1_moe_w_up, l1_moe_w_down, l1_ln2_g, l1_ln2_b)
    return x4.reshape(B, S, D)
```

```python
import functools

import jax
import jax.numpy as jnp
from jax import lax
from jax.experimental import pallas as pl
from jax.experimental.pallas import tpu as pltpu

HEAD_DIM = 128
POOL_WINDOWS = (2, 4, 8, 16)
POOL_HALO = 16
TOP_K = 2
LN_EPS = 1e-5
DEPTH = 2
ALPHA = (2 * DEPTH) ** 0.25
LANES = 128
DENSE_ROWS = 1024
DENSE_FF_TILE = 256
EXPERT_FF_TILE = 512
EXPERT_OUT_TILE = 512
EXPERT_TILE_ROWS = 1024
EXPERT_SUB_ROWS = 256
DMA_UNROLL = 8
EPI_COLS = 1024
EPI_ROWS = 128
VMEM_LIMIT = 56 * 1024 * 1024
EXP_UNDERFLOW = -104.0
MASKED_LOGIT = -1e30

BF16 = jnp.bfloat16
F32 = jnp.float32


def _params(n_axes):
    return pltpu.CompilerParams(dimension_semantics=("arbitrary",) * n_axes,
                                vmem_limit_bytes=VMEM_LIMIT)


def _layer_norm(t, g, b):
    mu = jnp.mean(t, axis=-1, keepdims=True)
    c = t - mu
    var = jnp.mean(c * c, axis=-1, keepdims=True)
    return c * lax.rsqrt(var + LN_EPS) * g + b


def _mm_kernel(x_ref, w_ref, o_ref, wb_ref, *, scaled_tiles, scale):
    @pl.when(pl.program_id(1) == 0)
    def _():
        wb_ref[...] = w_ref[...].astype(BF16)

    acc = jnp.dot(x_ref[...], wb_ref[...], preferred_element_type=F32)
    if scaled_tiles:
        acc = acc * jnp.where(pl.program_id(0) < scaled_tiles, scale, 1.0)
    o_ref[...] = acc.astype(o_ref.dtype)


def matmul(x, w, out_dtype, *, tm=1024, tn=512, scaled_cols=0, scale=1.0):
    M, K = x.shape
    N = w.shape[1]
    tm, tn = min(tm, M), min(tn, N)
    assert M % tm == 0 and N % tn == 0 and scaled_cols % tn == 0
    return pl.pallas_call(
        functools.partial(_mm_kernel, scaled_tiles=scaled_cols // tn, scale=scale),
        grid=(N // tn, M // tm),
        in_specs=[pl.BlockSpec((tm, K), lambda j, i: (i, 0)),
                  pl.BlockSpec((K, tn), lambda j, i: (0, j))],
        out_specs=pl.BlockSpec((tm, tn), lambda j, i: (i, j)),
        out_shape=jax.ShapeDtypeStruct((M, N), out_dtype),
        scratch_shapes=[pltpu.VMEM((K, tn), BF16)],
        compiler_params=_params(2),
        name="matmul",
    )(x, w)


def _mm_ln_kernel(x_ref, w_ref, res_hbm, g_ref, b_ref, o_ref, *rest, nk, k_tail):
    obf_ref = rest[0] if len(rest) == 3 else None
    res_ref, res_sem = rest[-2:]
    i, k = pl.program_id(0), pl.program_id(1)
    tm, D = o_ref.shape
    tk = x_ref.shape[1]
    res_copy = pltpu.make_async_copy(res_hbm.at[pl.ds(pl.multiple_of(i * tm, tm), tm), :], res_ref, res_sem)

    @pl.when(k == 0)
    def _():
        res_copy.start()
        o_ref[...] = jnp.zeros_like(o_ref)

    def accumulate(valid):
        x = x_ref[...]
        if valid < tk:
            keep_x = lax.broadcasted_iota(jnp.int32, (1, tk), 1) < valid
            keep_w = lax.broadcasted_iota(jnp.int32, (tk, 1), 0) < valid
            x = jnp.where(keep_x, x.astype(F32), 0.0).astype(BF16)
        for c in range(0, D, EPI_COLS):
            cs = slice(c, c + EPI_COLS)
            w = w_ref[:, cs]
            if valid < tk:
                w = jnp.where(keep_w, w.astype(F32), 0.0)
            o_ref[:, cs] += jnp.dot(x, w.astype(BF16), preferred_element_type=F32)

    if k_tail == tk:
        accumulate(tk)
    else:
        pl.when(k < nk - 1)(lambda: accumulate(tk))
        pl.when(k == nk - 1)(lambda: accumulate(k_tail))

    @pl.when(k == nk - 1)
    def _():
        res_copy.wait()
        for r in range(0, tm, EPI_ROWS):
            rs = slice(r, r + EPI_ROWS)
            y = _layer_norm(ALPHA * res_ref[rs, :] + o_ref[rs, :], g_ref[...], b_ref[...])
            o_ref[rs, :] = y
            if obf_ref is not None:
                obf_ref[rs, :] = y.astype(BF16)


def matmul_residual_ln(x, w, res, g, b, *, emit_bf16, tm=512, tk=512):
    M, K = x.shape
    D = w.shape[1]
    tm = min(tm, M)
    assert M % tm == 0
    nk = pl.cdiv(K, tk)
    row = pl.BlockSpec((tm, D), lambda i, k: (i, 0))
    vec = pl.BlockSpec((1, D), lambda i, k: (0, 0))
    return pl.pallas_call(
        functools.partial(_mm_ln_kernel, nk=nk, k_tail=K - (nk - 1) * tk),
        grid=(M // tm, nk),
        in_specs=[pl.BlockSpec((tm, tk), lambda i, k: (i, k)),
                  pl.BlockSpec((tk, D), lambda i, k: (k, 0)),
                  pl.BlockSpec(memory_space=pl.ANY),
                  vec, vec],
        out_specs=[row, row] if emit_bf16 else [row],
        out_shape=[jax.ShapeDtypeStruct((M, D), F32)] + [jax.ShapeDtypeStruct((M, D), BF16)] * emit_bf16,
        scratch_shapes=[pltpu.VMEM((tm, D), F32), pltpu.SemaphoreType.DMA(())],
        compiler_params=_params(2),
        name="matmul_residual_ln",
    )(x, w, res, g.reshape(1, D), b.reshape(1, D))


def _pool_kernel(u_ref, halo_ref, x_ref, wg_ref, sc_ref, g_ref, b_ref, o_ref, obf_ref, ext_ref, *, tm, C):
    i = pl.program_id(0)
    pos = i * tm + 1 + lax.broadcasted_iota(jnp.int32, (tm, 1), 0)
    for gi, w in enumerate(POOL_WINDOWS):
        cs = slice(gi * C, (gi + 1) * C)
        ext_ref[0:POOL_HALO, :] = jnp.where(i > 0, halo_ref[:, cs], 0.0)
        ext_ref[POOL_HALO:, :] = u_ref[:, cs]
        u = u_ref[:, cs]
        win = u
        for k in range(1, w):
            win = win + ext_ref[pl.ds(POOL_HALO - k, tm), :]
        count = jnp.minimum(pos, w).astype(F32)
        d = win / count - u
        y = jnp.dot(d.astype(BF16), wg_ref[gi], preferred_element_type=F32)
        o_ref[:, cs] = ALPHA * x_ref[:, cs] + y * sc_ref[:, cs]
    for r in range(0, tm, EPI_ROWS):
        rs = slice(r, r + EPI_ROWS)
        y = _layer_norm(o_ref[rs, :], g_ref[...], b_ref[...])
        o_ref[rs, :] = y
        obf_ref[rs, :] = y.astype(BF16)


def pool_mix_ln(u, x, w_group, scale, g, b, *, tm=256):
    S, D = u.shape
    G, C, _ = w_group.shape
    assert G == len(POOL_WINDOWS) and G * C == D and S % tm == 0 and tm % POOL_HALO == 0
    hb = tm // POOL_HALO
    row = pl.BlockSpec((tm, D), lambda i: (i, 0))
    vec = pl.BlockSpec((1, D), lambda i: (0, 0))
    return pl.pallas_call(
        functools.partial(_pool_kernel, tm=tm, C=C),
        grid=(S // tm,),
        in_specs=[row,
                  pl.BlockSpec((POOL_HALO, D), lambda i: (jnp.maximum(i * hb - 1, 0), 0)),
                  row,
                  pl.BlockSpec((G, C, C), lambda i: (0, 0, 0)),
                  vec, vec, vec],
        out_specs=[row, row],
        out_shape=[jax.ShapeDtypeStruct((S, D), F32), jax.ShapeDtypeStruct((S, D), BF16)],
        scratch_shapes=[pltpu.VMEM((tm + POOL_HALO, C), F32)],
        compiler_params=_params(1),
        name="pool_mix_ln",
    )(u, u, x, w_group, scale.reshape(1, D), g.reshape(1, D), b.reshape(1, D))


def _glu_math(x, wg, wu):
    gate = jnp.dot(x, wg, preferred_element_type=F32)
    up = jnp.dot(x, wu, preferred_element_type=F32)
    return (gate * jax.nn.sigmoid(gate) * up).astype(BF16)


def _sb_tile(q, ks, vs, tri2, drop_run, mask):
    tb = tri2.shape[1]
    z = lax.dot_general(q, ks, (((1,), (1,)), ((), ())), preferred_element_type=F32)
    if mask is not None:
        z = jnp.where(mask, z, MASKED_LOGIT)
    sp = jnp.log(1.0 + jnp.exp(-jnp.abs(z)))
    log_beta = jnp.minimum(z, 0.0) - sp
    drop = jnp.maximum(z, 0.0) + sp
    later = [None] * (z.shape[1] // tb)
    for blk in reversed(range(len(later))):
        d = drop[:, blk * tb:(blk + 1) * tb]
        hi = d.astype(BF16)
        lo = (d - hi.astype(F32)).astype(BF16)
        later[blk] = jnp.dot(jnp.concatenate([hi, lo], axis=1), tri2, preferred_element_type=F32) + drop_run
        drop_run = drop_run + jnp.sum(d, axis=1, keepdims=True)
    w = jnp.exp(log_beta - jnp.concatenate(later, axis=1))
    pv = jnp.dot(w.astype(BF16), vs, preferred_element_type=F32)
    return pv, drop_run


def _attn_kernel(q_ref, k_ref, v_ref, tri_ref, o_ref, *, tq, heads):
    i = pl.program_id(1)
    t0 = i * tq
    w0 = pl.multiple_of(jnp.maximum(t0 - tq, 0), tq)
    t_pos = t0 + lax.broadcasted_iota(jnp.int32, (tq, 1), 0)
    mask = (w0 + lax.broadcasted_iota(jnp.int32, (1, 2 * tq), 1)) < t_pos
    zero = jnp.zeros((tq, 1), F32)
    accs, runs = [], []
    for g in range(heads):
        hs = slice(g * HEAD_DIM, (g + 1) * HEAD_DIM)
        pv, run = _sb_tile(q_ref[:, hs], k_ref[pl.ds(w0, 2 * tq), hs], v_ref[pl.ds(w0, 2 * tq), hs],
                           tri_ref[...], zero, mask)
        accs.append(pv)
        runs.append(run)

    def alive(runs):
        return functools.reduce(jnp.minimum, [jnp.min(r) for r in runs]) < -EXP_UNDERFLOW

    def cond(carry):
        j, runs, _ = carry
        return jnp.logical_and(j >= 0, alive(runs))

    def body(carry):
        j, runs, accs = carry
        start = pl.multiple_of(j * tq, tq)
        new_runs, new_accs = [], []
        for g in range(heads):
            hs = slice(g * HEAD_DIM, (g + 1) * HEAD_DIM)
            pv, run = _sb_tile(q_ref[:, hs], k_ref[pl.ds(start, tq), hs], v_ref[pl.ds(start, tq), hs],
                               tri_ref[...], runs[g], None)
            new_accs.append(accs[g] + pv)
            new_runs.append(run)
        return j - 1, tuple(new_runs), tuple(new_accs)

    _, _, accs = lax.while_loop(cond, body, (w0 // tq - 1, tuple(runs), tuple(accs)))
    for g in range(heads):
        o_ref[:, g * HEAD_DIM:(g + 1) * HEAD_DIM] = accs[g].astype(o_ref.dtype)


def _stacked_tri(n):
    r = lax.broadcasted_iota(jnp.int32, (n, n), 0)
    c = lax.broadcasted_iota(jnp.int32, (n, n), 1)
    tri = (r > c).astype(BF16)
    return jnp.concatenate([tri, tri], axis=0)


def stick_breaking_attention(qkv, *, tq=256, heads=4):
    S, D3 = qkv.shape
    D = D3 // 3
    H = D // HEAD_DIM
    assert S % tq == 0 and S >= 2 * tq and H % heads == 0
    hw = heads * HEAD_DIM
    nb = D // hw
    return pl.pallas_call(
        functools.partial(_attn_kernel, tq=tq, heads=heads),
        grid=(nb, S // tq),
        in_specs=[pl.BlockSpec((tq, hw), lambda h, i: (i, h)),
                  pl.BlockSpec((S, hw), lambda h, i: (0, nb + h)),
                  pl.BlockSpec((S, hw), lambda h, i: (0, 2 * nb + h)),
                  pl.BlockSpec((2 * tq, tq), lambda h, i: (0, 0))],
        out_specs=pl.BlockSpec((tq, hw), lambda h, i: (i, h)),
        out_shape=jax.ShapeDtypeStruct((S, D), BF16),
        compiler_params=_params(2),
        name="stick_breaking_attention",
    )(qkv, qkv, qkv, _stacked_tri(tq))


def _router_kernel(h_ref, w_ref, o_ref, *, n_experts):
    h = h_ref[...]
    w = w_ref[...]
    h1 = h.astype(BF16)
    h2 = (h - h1.astype(F32)).astype(BF16)
    h3 = (h - h1.astype(F32) - h2.astype(F32)).astype(BF16)
    w1 = w.astype(BF16)
    w2 = (w - w1.astype(F32)).astype(BF16)
    w3 = (w - w1.astype(F32) - w2.astype(F32)).astype(BF16)
    dot = functools.partial(jnp.dot, preferred_element_type=F32)
    logits = (dot(h3, w1) + dot(h2, w2) + dot(h1, w3)) + (dot(h2, w1) + dot(h1, w2)) + dot(h1, w1)
    lane = lax.broadcasted_iota(jnp.int32, logits.shape, 1)
    lane_f = lane.astype(F32)
    neg = jnp.float32(-jnp.inf)
    logits = jnp.where(lane < n_experts, logits, neg)
    m1 = jnp.max(logits, axis=1, keepdims=True)
    i1 = jnp.min(jnp.where(logits == m1, lane_f, float(LANES)), axis=1, keepdims=True)
    rest = jnp.where(lane_f == i1, neg, logits)
    m2 = jnp.max(rest, axis=1, keepdims=True)
    i2 = jnp.min(jnp.where(rest == m2, lane_f, float(LANES)), axis=1, keepdims=True)
    e2 = jnp.exp(m2 - m1)
    g1 = 1.0 / (1.0 + e2)
    g2 = e2 / (1.0 + e2)
    out = jnp.where(lane == 0, i1, 0.0)
    out = jnp.where(lane == 1, i2, out)
    out = jnp.where(lane == 2, g1, out)
    out = jnp.where(lane == 3, g2, out)
    o_ref[...] = out


def route_top2(h, w_router, *, tm=256):
    S, D = h.shape
    E = w_router.shape[1]
    assert E <= LANES and S % tm == 0
    w_pad = jnp.pad(w_router, ((0, 0), (0, LANES - E)))
    info = pl.pallas_call(
        functools.partial(_router_kernel, n_experts=E),
        grid=(S // tm,),
        in_specs=[pl.BlockSpec((tm, D), lambda i: (i, 0)),
                  pl.BlockSpec((D, LANES), lambda i: (0, 0))],
        out_specs=pl.BlockSpec((tm, LANES), lambda i: (i, 0)),
        out_shape=jax.ShapeDtypeStruct((S, LANES), F32),
        compiler_params=_params(1),
        name="route_top2",
    )(h, w_pad)
    return info[:, 0:TOP_K].astype(jnp.int32), info[:, TOP_K:2 * TOP_K]


def _gather_kernel(src_ref, live_ref, h_ref, o_ref, stage_ref, sems, *, tg):
    i, n = pl.program_id(0), pl.num_programs(0)
    slot = i % 2

    def row_copy(src_row, r, slot):
        return pltpu.make_async_copy(h_ref.at[pl.ds(src_row, 1), :],
                                     stage_ref.at[slot, pl.ds(r, 1), :], sems.at[slot])

    def issue(step, slot):
        def start_row(r, c):
            row_copy(src_ref[step * tg + r], r, slot).start()
            return c
        lax.fori_loop(0, tg, start_row, 0, unroll=DMA_UNROLL)

    @pl.when(jnp.logical_and(i == 0, live_ref[0] == 1))
    def _():
        issue(0, 0)

    @pl.when(jnp.logical_and(i + 1 < n, live_ref[jnp.minimum(i + 1, n - 1)] == 1))
    def _():
        issue(i + 1, 1 - slot)

    @pl.when(live_ref[i] == 1)
    def _():
        def wait_row(r, c):
            row_copy(0, r, slot).wait()
            return c
        lax.fori_loop(0, tg, wait_row, 0, unroll=DMA_UNROLL)
        o_ref[...] = stage_ref[slot].astype(BF16)

    @pl.when(live_ref[i] == 0)
    def _():
        o_ref[...] = jnp.zeros_like(o_ref)


def gather_rows_bf16(h, src_rows, tile_live, *, tg):
    S, D = h.shape
    N = src_rows.shape[0]
    assert N % tg == 0 and tile_live.shape[0] == N // tg
    return pl.pallas_call(
        functools.partial(_gather_kernel, tg=tg),
        grid_spec=pltpu.PrefetchScalarGridSpec(
            num_scalar_prefetch=2,
            grid=(N // tg,),
            in_specs=[pl.BlockSpec(memory_space=pl.ANY)],
            out_specs=pl.BlockSpec((tg, D), lambda i, src, live: (i, 0)),
            scratch_shapes=[pltpu.VMEM((2, tg, D), F32), pltpu.SemaphoreType.DMA((2,))]),
        out_shape=jax.ShapeDtypeStruct((N, D), BF16),
        compiler_params=_params(1),
        name="gather_rows_bf16",
    )(src_rows, tile_live, h)


def _run_schedule(tile_expert, used):
    T = tile_expert.shape[0]
    t = jnp.arange(T, dtype=jnp.int32)
    prev = jnp.concatenate([tile_expert[:1], tile_expert[:-1]])
    first = jnp.logical_and(jnp.logical_or(t == 0, tile_expert != prev), t < used[0])
    run = jnp.cumsum(first.astype(jnp.int32)) - 1
    later_first = jnp.logical_and(first[None, :], t[None, :] > t[:, None])
    nxt_tile = jnp.min(jnp.where(later_first, t[None, :], T), axis=1)
    nxt = jnp.where(nxt_tile < T, tile_expert[jnp.minimum(nxt_tile, T - 1)], tile_expert[0])
    counts = jnp.stack([used[0], jnp.sum(first.astype(jnp.int32))])
    return first.astype(jnp.int32), run.astype(jnp.int32), nxt.astype(jnp.int32), counts.astype(jnp.int32)


def _stream_run_weights(sched, copies, on_arrival):
    te_ref, first_ref, run_ref, nxt_ref, cnt_ref = sched
    c, t, nc = pl.program_id(0), pl.program_id(1), pl.num_programs(0)
    n_runs = cnt_ref[1]

    @pl.when(first_ref[t] == 1)
    def _():
        @pl.when(jnp.logical_and(c == 0, run_ref[t] == 0))
        def _():
            for cp in copies(te_ref[t], c):
                cp.start()

        for cp in copies(te_ref[t], c):
            cp.wait()
        on_arrival()
        next_c = c + (run_ref[t] == n_runs - 1).astype(jnp.int32)

        @pl.when(next_c < nc)
        def _():
            for cp in copies(nxt_ref[t], next_c):
                cp.start()


def _grouped_kernel(te_ref, first_ref, run_ref, nxt_ref, cnt_ref, valid_ref, x_ref, *rest, n_w, sub, math):
    w_hbm, o_ref = rest[:n_w], rest[n_w]
    stage_ref, wb_ref, sems = rest[n_w + 1:]
    t = pl.program_id(1)
    tm, tc = o_ref.shape

    def copies(e, c):
        cols = pl.ds(pl.multiple_of(c * tc, tc), tc)
        return [pltpu.make_async_copy(w_hbm[k].at[e, :, cols], stage_ref.at[k], sems.at[k])
                for k in range(n_w)]

    def cast():
        for k in range(n_w):
            wb_ref[k] = stage_ref[k].astype(BF16)

    _stream_run_weights((te_ref, first_ref, run_ref, nxt_ref, cnt_ref), copies, cast)

    for q in range(tm // sub):
        rs = slice(q * sub, (q + 1) * sub)

        @pl.when(q < valid_ref[t])
        def _():
            o_ref[rs, :] = math(x_ref[rs, :], *[wb_ref[k] for k in range(n_w)])

        @pl.when(q >= valid_ref[t])
        def _():
            o_ref[rs, :] = jnp.zeros((sub, tc), o_ref.dtype)


def grouped_matmul(x, ws, tile_expert, used, valid, math, out_dtype, *, tm, sub, tc):
    N, K = x.shape
    E, _, C = ws[0].shape
    n_w = len(ws)
    assert N % tm == 0 and C % tc == 0 and tm % sub == 0
    sched = (tile_expert,) + _run_schedule(tile_expert, used) + (valid,)
    return pl.pallas_call(
        functools.partial(_grouped_kernel, n_w=n_w, sub=sub, math=math),
        grid_spec=pltpu.PrefetchScalarGridSpec(
            num_scalar_prefetch=len(sched),
            grid=(C // tc, N // tm),
            in_specs=[pl.BlockSpec((tm, K), lambda c, t, *_: (t, 0))] + [pl.BlockSpec(memory_space=pl.ANY)] * n_w,
            out_specs=pl.BlockSpec((tm, tc), lambda c, t, *_: (t, c)),
            scratch_shapes=[pltpu.VMEM((n_w, K, tc), F32), pltpu.VMEM((n_w, K, tc), BF16),
                            pltpu.SemaphoreType.DMA((n_w,))]),
        out_shape=jax.ShapeDtypeStruct((N, C), out_dtype),
        compiler_params=_params(2),
        name="grouped_matmul",
    )(*sched, x, *ws)


def _down_math(x, w):
    return jnp.dot(x, w, preferred_element_type=F32)


def _combine_kernel(pos_ref, y_ref, h_ref, gate_ref, g_ref, b_ref, o_ref, stage_ref, sems, *, tc):
    i = pl.program_id(0)
    n = pl.num_programs(0)
    slot = i % 2

    def row_copy(y_row, r, k, slot):
        return pltpu.make_async_copy(y_ref.at[pl.ds(y_row, 1), :],
                                     stage_ref.at[slot, k, pl.ds(r, 1), :], sems.at[slot])

    def issue(step, slot):
        def start_row(r, c):
            for k in range(TOP_K):
                row_copy(pos_ref[(step * tc + r) * TOP_K + k], r, k, slot).start()
            return c
        lax.fori_loop(0, tc, start_row, 0, unroll=DMA_UNROLL)

    @pl.when(i == 0)
    def _():
        issue(0, 0)

    @pl.when(i + 1 < n)
    def _():
        issue(i + 1, 1 - slot)

    def wait_row(r, c):
        for k in range(TOP_K):
            row_copy(0, r, k, slot).wait()
        return c
    lax.fori_loop(0, tc, wait_row, 0, unroll=DMA_UNROLL)

    for r in range(0, tc, EPI_ROWS):
        rs = slice(r, r + EPI_ROWS)
        gates = gate_ref[rs, :]
        moe = gates[:, 0:1] * stage_ref[slot, 0, rs, :]
        for k in range(1, TOP_K):
            moe = moe + gates[:, k:k + 1] * stage_ref[slot, k, rs, :]
        o_ref[rs, :] = _layer_norm(ALPHA * h_ref[rs, :] + moe, g_ref[...], b_ref[...])


def combine_ln(y, pos, h, gates, g, b, *, tc=256):
    S, D = h.shape
    assert S % tc == 0
    row = pl.BlockSpec((tc, D), lambda i, pos: (i, 0))
    vec = pl.BlockSpec((1, D), lambda i, pos: (0, 0))
    return pl.pallas_call(
        functools.partial(_combine_kernel, tc=tc),
        grid_spec=pltpu.PrefetchScalarGridSpec(
            num_scalar_prefetch=1,
            grid=(S // tc,),
            in_specs=[pl.BlockSpec(memory_space=pl.ANY), row,
                      pl.BlockSpec((tc, TOP_K), lambda i, pos: (i, 0)), vec, vec],
            out_specs=row,
            scratch_shapes=[pltpu.VMEM((2, TOP_K, tc, D), F32), pltpu.SemaphoreType.DMA((2,))]),
        out_shape=jax.ShapeDtypeStruct((S, D), F32),
        compiler_params=_params(1),
        name="combine_ln",
    )(pos.reshape(-1), y, h, gates, g.reshape(1, D), b.reshape(1, D))


def _routing_tables(idx, n_experts, tile, sub):
    S, K = idx.shape
    flat = idx.reshape(-1)
    onehot = (flat[:, None] == jnp.arange(n_experts, dtype=jnp.int32)[None, :]).astype(jnp.int32)
    csum = jnp.cumsum(onehot, axis=0)
    rank = jnp.sum(onehot * csum, axis=1) - 1
    counts = csum[-1]
    padded = (counts + tile - 1) // tile * tile
    ends = jnp.cumsum(padded)
    starts = ends - padded
    pos = jnp.sum(onehot * starts[None, :], axis=1) + rank
    n_rows = S * K + n_experts * tile
    src = jnp.zeros((n_rows,), jnp.int32).at[pos].set(jnp.arange(S * K, dtype=jnp.int32) // K)
    used = ends[-1] // tile
    t = jnp.arange(n_rows // tile, dtype=jnp.int32)
    tile_expert = jnp.sum((jnp.minimum(t, used - 1)[:, None] * tile >= ends[None, :]).astype(jnp.int32), axis=1)
    rows_in_tile = jnp.clip(counts[tile_expert] - (t * tile - starts[tile_expert]), 0, tile)
    valid = jnp.where(t < used, (rows_in_tile + sub - 1) // sub, 0)
    per = tile // sub
    sub_live = (jnp.arange(n_rows // sub, dtype=jnp.int32) % per) < jnp.repeat(valid, per)
    i32 = lambda a: a.astype(jnp.int32)
    return pos.reshape(S, K), src, i32(tile_expert), i32(used.reshape(1)), i32(valid), i32(sub_live)


def moe_ln(h, w_router, wg, wu, wd, g, b):
    E = w_router.shape[1]
    tm, sub = EXPERT_TILE_ROWS, EXPERT_SUB_ROWS
    idx, gates = route_top2(h, w_router)
    pos, src, tile_expert, used, valid, sub_live = _routing_tables(idx, E, tm, sub)
    xg = gather_rows_bf16(h, src, sub_live, tg=sub)
    hid = grouped_matmul(xg, (wg, wu), tile_expert, used, valid, _glu_math, BF16,
                         tm=tm, sub=sub, tc=EXPERT_FF_TILE)
    y = grouped_matmul(hid, (wd,), tile_expert, used, valid, _down_math, F32,
                       tm=tm, sub=sub, tc=EXPERT_OUT_TILE)
    return combine_ln(y, pos, h, gates, g, b)


def kernel(x, l0_pool_w_in, l0_pool_w_group, l0_pool_scale, l0_ln1_g, l0_ln1_b, l0_ffn_w_gate, l0_ffn_w_up, l0_ffn_w_down, l0_ln2_g, l0_ln2_b, l1_attn_w_qkv, l1_attn_w_o, l1_ln1_g, l1_ln1_b, l1_moe_w_router, l1_moe_w_gate, l1_moe_w_up, l1_moe_w_down, l1_ln2_g, l1_ln2_b):
    B, S, D = x.shape
    x0 = x.reshape(B * S, D)
    assert B == 1, "sequence mixing below treats all rows as one sequence"
    bf = lambda w: w.astype(BF16)

    u = matmul(bf(x0), l0_pool_w_in, F32)
    x1, x1b = pool_mix_ln(u, x0, bf(l0_pool_w_group), l0_pool_scale, l0_ln1_g, l0_ln1_b)
    n_tiles = (B * S) // DENSE_ROWS
    hid = grouped_matmul(x1b, (l0_ffn_w_gate[None], l0_ffn_w_up[None]), jnp.zeros((n_tiles,), jnp.int32),
                         jnp.full((1,), n_tiles, jnp.int32), jnp.ones((n_tiles,), jnp.int32), _glu_math, BF16,
                         tm=DENSE_ROWS, sub=DENSE_ROWS, tc=DENSE_FF_TILE)
    x2, x2b = matmul_residual_ln(hid, bf(l0_ffn_w_down), x1, l0_ln2_g, l0_ln2_b, emit_bf16=True)

    qkv = matmul(x2b, l1_attn_w_qkv, BF16, scaled_cols=D, scale=HEAD_DIM ** -0.5)
    o = stick_breaking_attention(qkv)
    (x3,) = matmul_residual_ln(o, l1_attn_w_o, x2, l1_ln1_g, l1_ln1_b, emit_bf16=False)
    x4 = moe_ln(x3, l1_moe_w_router, l1_moe_w_gate, l1_moe_w_up, l1_moe_w_down, l1_ln2_g, l1_ln2_b)
    return x4.reshape(B, S, D)
```

```python
import functools

import jax
import jax.numpy as jnp
from jax import lax
from jax.experimental import pallas as pl
from jax.experimental.pallas import tpu as pltpu

HEAD_DIM = 128
POOL_WINDOWS = (2, 4, 8, 16)
POOL_HALO = 16
TOP_K = 2
LN_EPS = 1e-5
DEPTH = 2
ALPHA = (2 * DEPTH) ** 0.25
LANES = 128
DENSE_ROWS = 1024
DENSE_FF_TILE = 256
EXPERT_FF_TILE = 512
EXPERT_OUT_TILE = 512
EXPERT_TILE_ROWS = 1024
EXPERT_SUB_ROWS = 256
DMA_UNROLL = 8
EPI_COLS = 1024
EPI_ROWS = 128
VMEM_LIMIT = 56 * 1024 * 1024
EXP_UNDERFLOW = -104.0
MASKED_LOGIT = -1e30

BF16 = jnp.bfloat16
F32 = jnp.float32


def _params(n_axes):
    return pltpu.CompilerParams(dimension_semantics=("arbitrary",) * n_axes,
                                vmem_limit_bytes=VMEM_LIMIT)


def _layer_norm(t, g, b):
    mu = jnp.mean(t, axis=-1, keepdims=True)
    c = t - mu
    var = jnp.mean(c * c, axis=-1, keepdims=True)
    return c * lax.rsqrt(var + LN_EPS) * g + b


def _mm_kernel(x_ref, w_ref, o_ref, wb_ref, *, scaled_tiles, scale):
    @pl.when(pl.program_id(1) == 0)
    def _():
        wb_ref[...] = w_ref[...].astype(BF16)

    acc = jnp.dot(x_ref[...], wb_ref[...], preferred_element_type=F32)
    if scaled_tiles:
        acc = acc * jnp.where(pl.program_id(0) < scaled_tiles, scale, 1.0)
    o_ref[...] = acc.astype(o_ref.dtype)


def matmul(x, w, out_dtype, *, tm=1024, tn=512, scaled_cols=0, scale=1.0):
    M, K = x.shape
    N = w.shape[1]
    tm, tn = min(tm, M), min(tn, N)
    assert M % tm == 0 and N % tn == 0 and scaled_cols % tn == 0
    return pl.pallas_call(
        functools.partial(_mm_kernel, scaled_tiles=scaled_cols // tn, scale=scale),
        grid=(N // tn, M // tm),
        in_specs=[pl.BlockSpec((tm, K), lambda j, i: (i, 0)),
                  pl.BlockSpec((K, tn), lambda j, i: (0, j))],
        out_specs=pl.BlockSpec((tm, tn), lambda j, i: (i, j)),
        out_shape=jax.ShapeDtypeStruct((M, N), out_dtype),
        scratch_shapes=[pltpu.VMEM((K, tn), BF16)],
        compiler_params=_params(2),
        name="matmul",
    )(x, w)


def _mm_ln_kernel(x_ref, w_ref, res_hbm, g_ref, b_ref, o_ref, *rest, nk, k_tail):
    obf_ref = rest[0] if len(rest) == 3 else None
    res_ref, res_sem = rest[-2:]
    i, k = pl.program_id(0), pl.program_id(1)
    tm, D = o_ref.shape
    tk = x_ref.shape[1]
    res_copy = pltpu.make_async_copy(res_hbm.at[pl.ds(pl.multiple_of(i * tm, tm), tm), :], res_ref, res_sem)

    @pl.when(k == 0)
    def _():
        res_copy.start()
        o_ref[...] = jnp.zeros_like(o_ref)

    def accumulate(valid):
        x = x_ref[...]
        if valid < tk:
            keep_x = lax.broadcasted_iota(jnp.int32, (1, tk), 1) < valid
            keep_w = lax.broadcasted_iota(jnp.int32, (tk, 1), 0) < valid
            x = jnp.where(keep_x, x.astype(F32), 0.0).astype(BF16)
        for c in range(0, D, EPI_COLS):
            cs = slice(c, c + EPI_COLS)
            w = w_ref[:, cs]
            if valid < tk:
                w = jnp.where(keep_w, w.astype(F32), 0.0)
            o_ref[:, cs] += jnp.dot(x, w.astype(BF16), preferred_element_type=F32)

    if k_tail == tk:
        accumulate(tk)
    else:
        pl.when(k < nk - 1)(lambda: accumulate(tk))
        pl.when(k == nk - 1)(lambda: accumulate(k_tail))

    @pl.when(k == nk - 1)
    def _():
        res_copy.wait()
        for r in range(0, tm, EPI_ROWS):
            rs = slice(r, r + EPI_ROWS)
            y = _layer_norm(ALPHA * res_ref[rs, :] + o_ref[rs, :], g_ref[...], b_ref[...])
            o_ref[rs, :] = y
            if obf_ref is not None:
                obf_ref[rs, :] = y.astype(BF16)


def matmul_residual_ln(x, w, res, g, b, *, emit_bf16, tm=512, tk=512):
    M, K = x.shape
    D = w.shape[1]
    tm = min(tm, M)
    assert M % tm == 0
    nk = pl.cdiv(K, tk)
    row = pl.BlockSpec((tm, D), lambda i, k: (i, 0))
    vec = pl.BlockSpec((1, D), lambda i, k: (0, 0))
    return pl.pallas_call(
        functools.partial(_mm_ln_kernel, nk=nk, k_tail=K - (nk - 1) * tk),
        grid=(M // tm, nk),
        in_specs=[pl.BlockSpec((tm, tk), lambda i, k: (i, k)),
                  pl.BlockSpec((tk, D), lambda i, k: (k, 0)),
                  pl.BlockSpec(memory_space=pl.ANY),
                  vec, vec],
        out_specs=[row, row] if emit_bf16 else [row],
        out_shape=[jax.ShapeDtypeStruct((M, D), F32)] + [jax.ShapeDtypeStruct((M, D), BF16)] * emit_bf16,
        scratch_shapes=[pltpu.VMEM((tm, D), F32), pltpu.SemaphoreType.DMA(())],
        compiler_params=_params(2),
        name="matmul_residual_ln",
    )(x, w, res, g.reshape(1, D), b.reshape(1, D))


def _pool_kernel(u_ref, halo_ref, x_ref, wg_ref, sc_ref, g_ref, b_ref, o_ref, obf_ref, ext_ref, *, tm, C):
    i = pl.program_id(0)
    pos = i * tm + 1 + lax.broadcasted_iota(jnp.int32, (tm, 1), 0)
    for gi, w in enumerate(POOL_WINDOWS):
        cs = slice(gi * C, (gi + 1) * C)
        ext_ref[0:POOL_HALO, :] = jnp.where(i > 0, halo_ref[:, cs], 0.0)
        ext_ref[POOL_HALO:, :] = u_ref[:, cs]
        u = u_ref[:, cs]
        win = u
        for k in range(1, w):
            win = win + ext_ref[pl.ds(POOL_HALO - k, tm), :]
        count = jnp.minimum(pos, w).astype(F32)
        d = win / count - u
        y = jnp.dot(d.astype(BF16), wg_ref[gi], preferred_element_type=F32)
        o_ref[:, cs] = ALPHA * x_ref[:, cs] + y * sc_ref[:, cs]
    for r in range(0, tm, EPI_ROWS):
        rs = slice(r, r + EPI_ROWS)
        y = _layer_norm(o_ref[rs, :], g_ref[...], b_ref[...])
        o_ref[rs, :] = y
        obf_ref[rs, :] = y.astype(BF16)


def pool_mix_ln(u, x, w_group, scale, g, b, *, tm=256):
    S, D = u.shape
    G, C, _ = w_group.shape
    assert G == len(POOL_WINDOWS) and G * C == D and S % tm == 0 and tm % POOL_HALO == 0
    hb = tm // POOL_HALO
    row = pl.BlockSpec((tm, D), lambda i: (i, 0))
    vec = pl.BlockSpec((1, D), lambda i: (0, 0))
    return pl.pallas_call(
        functools.partial(_pool_kernel, tm=tm, C=C),
        grid=(S // tm,),
        in_specs=[row,
                  pl.BlockSpec((POOL_HALO, D), lambda i: (jnp.maximum(i * hb - 1, 0), 0)),
                  row,
                  pl.BlockSpec((G, C, C), lambda i: (0, 0, 0)),
                  vec, vec, vec],
        out_specs=[row, row],
        out_shape=[jax.ShapeDtypeStruct((S, D), F32), jax.ShapeDtypeStruct((S, D), BF16)],
        scratch_shapes=[pltpu.VMEM((tm + POOL_HALO, C), F32)],
        compiler_params=_params(1),
        name="pool_mix_ln",
    )(u, u, x, w_group, scale.reshape(1, D), g.reshape(1, D), b.reshape(1, D))


def _glu_math(x, wg, wu):
    gate = jnp.dot(x, wg, preferred_element_type=F32)
    up = jnp.dot(x, wu, preferred_element_type=F32)
    return (gate * jax.nn.sigmoid(gate) * up).astype(BF16)


def _sb_tile(q, ks, vs, tri2, drop_run, mask):
    tb = tri2.shape[1]
    z = lax.dot_general(q, ks, (((1,), (1,)), ((), ())), preferred_element_type=F32)
    if mask is not None:
        z = jnp.where(mask, z, MASKED_LOGIT)
    sp = jnp.log(1.0 + jnp.exp(-jnp.abs(z)))
    log_beta = jnp.minimum(z, 0.0) - sp
    drop = jnp.maximum(z, 0.0) + sp
    later = [None] * (z.shape[1] // tb)
    for blk in reversed(range(len(later))):
        d = drop[:, blk * tb:(blk + 1) * tb]
        hi = d.astype(BF16)
        lo = (d - hi.astype(F32)).astype(BF16)
        later[blk] = jnp.dot(jnp.concatenate([hi, lo], axis=1), tri2, preferred_element_type=F32) + drop_run
        drop_run = drop_run + jnp.sum(d, axis=1, keepdims=True)
    w = jnp.exp(log_beta - jnp.concatenate(later, axis=1))
    pv = jnp.dot(w.astype(BF16), vs, preferred_element_type=F32)
    return pv, drop_run


def _attn_kernel(q_ref, k_ref, v_ref, tri_ref, o_ref, *, tq, heads):
    i = pl.program_id(1)
    t0 = i * tq
    w0 = pl.multiple_of(jnp.maximum(t0 - tq, 0), tq)
    t_pos = t0 + lax.broadcasted_iota(jnp.int32, (tq, 1), 0)
    mask = (w0 + lax.broadcasted_iota(jnp.int32, (1, 2 * tq), 1)) < t_pos
    zero = jnp.zeros((tq, 1), F32)
    accs, runs = [], []
    for g in range(heads):
        hs = slice(g * HEAD_DIM, (g + 1) * HEAD_DIM)
        pv, run = _sb_tile(q_ref[:, hs], k_ref[pl.ds(w0, 2 * tq), hs], v_ref[pl.ds(w0, 2 * tq), hs],
                           tri_ref[...], zero, mask)
        accs.append(pv)
        runs.append(run)

    def alive(runs):
        return functools.reduce(jnp.minimum, [jnp.min(r) for r in runs]) < -EXP_UNDERFLOW

    def cond(carry):
        j, runs, _ = carry
        return jnp.logical_and(j >= 0, alive(runs))

    def body(carry):
        j, runs, accs = carry
        start = pl.multiple_of(j * tq, tq)
        new_runs, new_accs = [], []
        for g in range(heads):
            hs = slice(g * HEAD_DIM, (g + 1) * HEAD_DIM)
            pv, run = _sb_tile(q_ref[:, hs], k_ref[pl.ds(start, tq), hs], v_ref[pl.ds(start, tq), hs],
                               tri_ref[...], runs[g], None)
            new_accs.append(accs[g] + pv)
            new_runs.append(run)
        return j - 1, tuple(new_runs), tuple(new_accs)

    _, _, accs = lax.while_loop(cond, body, (w0 // tq - 1, tuple(runs), tuple(accs)))
    for g in range(heads):
        o_ref[:, g * HEAD_DIM:(g + 1) * HEAD_DIM] = accs[g].astype(o_ref.dtype)


def _stacked_tri(n):
    r = lax.broadcasted_iota(jnp.int32, (n, n), 0)
    c = lax.broadcasted_iota(jnp.int32, (n, n), 1)
    tri = (r > c).astype(BF16)
    return jnp.concatenate([tri, tri], axis=0)


def stick_breaking_attention(qkv, *, tq=256, heads=4):
    S, D3 = qkv.shape
    D = D3 // 3
    H = D // HEAD_DIM
    assert S % tq == 0 and S >= 2 * tq and H % heads == 0
    hw = heads * HEAD_DIM
    nb = D // hw
    return pl.pallas_call(
        functools.partial(_attn_kernel, tq=tq, heads=heads),
        grid=(nb, S // tq),
        in_specs=[pl.BlockSpec((tq, hw), lambda h, i: (i, h)),
                  pl.BlockSpec((S, hw), lambda h, i: (0, nb + h)),
                  pl.BlockSpec((S, hw), lambda h, i: (0, 2 * nb + h)),
                  pl.BlockSpec((2 * tq, tq), lambda h, i: (0, 0))],
        out_specs=pl.BlockSpec((tq, hw), lambda h, i: (i, h)),
        out_shape=jax.ShapeDtypeStruct((S, D), BF16),
        compiler_params=_params(2),
        name="stick_breaking_attention",
    )(qkv, qkv, qkv, _stacked_tri(tq))


def _router_kernel(h_ref, w_ref, o_ref, *, n_experts):
    h = h_ref[...]
    w = w_ref[...]
    h1 = h.astype(BF16)
    h2 = (h - h1.astype(F32)).astype(BF16)
    h3 = (h - h1.astype(F32) - h2.astype(F32)).astype(BF16)
    w1 = w.astype(BF16)
    w2 = (w - w1.astype(F32)).astype(BF16)
    w3 = (w - w1.astype(F32) - w2.astype(F32)).astype(BF16)
    dot = functools.partial(jnp.dot, preferred_element_type=F32)
    logits = (dot(h3, w1) + dot(h2, w2) + dot(h1, w3)) + (dot(h2, w1) + dot(h1, w2)) + dot(h1, w1)
    lane = lax.broadcasted_iota(jnp.int32, logits.shape, 1)
    lane_f = lane.astype(F32)
    neg = jnp.float32(-jnp.inf)
    logits = jnp.where(lane < n_experts, logits, neg)
    m1 = jnp.max(logits, axis=1, keepdims=True)
    i1 = jnp.min(jnp.where(logits == m1, lane_f, float(LANES)), axis=1, keepdims=True)
    rest = jnp.where(lane_f == i1, neg, logits)
    m2 = jnp.max(rest, axis=1, keepdims=True)
    i2 = jnp.min(jnp.where(rest == m2, lane_f, float(LANES)), axis=1, keepdims=True)
    e2 = jnp.exp(m2 - m1)
    g1 = 1.0 / (1.0 + e2)
    g2 = e2 / (1.0 + e2)
    out = jnp.where(lane == 0, i1, 0.0)
    out = jnp.where(lane == 1, i2, out)
    out = jnp.where(lane == 2, g1, out)
    out = jnp.where(lane == 3, g2, out)
    o_ref[...] = out


def route_top2(h, w_router, *, tm=256):
    S, D = h.shape
    E = w_router.shape[1]
    assert E <= LANES and S % tm == 0
    w_pad = jnp.pad(w_router, ((0, 0), (0, LANES - E)))
    info = pl.pallas_call(
        functools.partial(_router_kernel, n_experts=E),
        grid=(S // tm,),
        in_specs=[pl.BlockSpec((tm, D), lambda i: (i, 0)),
                  pl.BlockSpec((D, LANES), lambda i: (0, 0))],
        out_specs=pl.BlockSpec((tm, LANES), lambda i: (i, 0)),
        out_shape=jax.ShapeDtypeStruct((S, LANES), F32),
        compiler_params=_params(1),
        name="route_top2",
    )(h, w_pad)
    return info[:, 0:TOP_K].astype(jnp.int32), info[:, TOP_K:2 * TOP_K]


def _gather_kernel(src_ref, live_ref, h_ref, o_ref, stage_ref, sems, *, tg):
    i, n = pl.program_id(0), pl.num_programs(0)
    slot = i % 2

    def row_copy(src_row, r, slot):
        return pltpu.make_async_copy(h_ref.at[pl.ds(src_row, 1), :],
                                     stage_ref.at[slot, pl.ds(r, 1), :], sems.at[slot])

    def issue(step, slot):
        def start_row(r, c):
            row_copy(src_ref[step * tg + r], r, slot).start()
            return c
        lax.fori_loop(0, tg, start_row, 0, unroll=DMA_UNROLL)

    @pl.when(jnp.logical_and(i == 0, live_ref[0] == 1))
    def _():
        issue(0, 0)

    @pl.when(jnp.logical_and(i + 1 < n, live_ref[jnp.minimum(i + 1, n - 1)] == 1))
    def _():
        issue(i + 1, 1 - slot)

    @pl.when(live_ref[i] == 1)
    def _():
        def wait_row(r, c):
            row_copy(0, r, slot).wait()
            return c
        lax.fori_loop(0, tg, wait_row, 0, unroll=DMA_UNROLL)
        o_ref[...] = stage_ref[slot].astype(BF16)

    @pl.when(live_ref[i] == 0)
    def _():
        o_ref[...] = jnp.zeros_like(o_ref)


def gather_rows_bf16(h, src_rows, tile_live, *, tg):
    S, D = h.shape
    N = src_rows.shape[0]
    assert N % tg == 0 and tile_live.shape[0] == N // tg
    return pl.pallas_call(
        functools.partial(_gather_kernel, tg=tg),
        grid_spec=pltpu.PrefetchScalarGridSpec(
            num_scalar_prefetch=2,
            grid=(N // tg,),
            in_specs=[pl.BlockSpec(memory_space=pl.ANY)],
            out_specs=pl.BlockSpec((tg, D), lambda i, src, live: (i, 0)),
            scratch_shapes=[pltpu.VMEM((2, tg, D), F32), pltpu.SemaphoreType.DMA((2,))]),
        out_shape=jax.ShapeDtypeStruct((N, D), BF16),
        compiler_params=_params(1),
        name="gather_rows_bf16",
    )(src_rows, tile_live, h)


def _run_schedule(tile_expert, used):
    T = tile_expert.shape[0]
    t = jnp.arange(T, dtype=jnp.int32)
    prev = jnp.concatenate([tile_expert[:1], tile_expert[:-1]])
    first = jnp.logical_and(jnp.logical_or(t == 0, tile_expert != prev), t < used[0])
    run = jnp.cumsum(first.astype(jnp.int32)) - 1
    later_first = jnp.logical_and(first[None, :], t[None, :] > t[:, None])
    nxt_tile = jnp.min(jnp.where(later_first, t[None, :], T), axis=1)
    nxt = jnp.where(nxt_tile < T, tile_expert[jnp.minimum(nxt_tile, T - 1)], tile_expert[0])
    counts = jnp.stack([used[0], jnp.sum(first.astype(jnp.int32))])
    return first.astype(jnp.int32), run.astype(jnp.int32), nxt.astype(jnp.int32), counts.astype(jnp.int32)


def _stream_run_weights(sched, copies, on_arrival):
    te_ref, first_ref, run_ref, nxt_ref, cnt_ref = sched
    c, t, nc = pl.program_id(0), pl.program_id(1), pl.num_programs(0)
    n_runs = cnt_ref[1]

    @pl.when(first_ref[t] == 1)
    def _():
        @pl.when(jnp.logical_and(c == 0, run_ref[t] == 0))
        def _():
            for cp in copies(te_ref[t], c):
                cp.start()

        for cp in copies(te_ref[t], c):
            cp.wait()
        on_arrival()
        next_c = c + (run_ref[t] == n_runs - 1).astype(jnp.int32)

        @pl.when(next_c < nc)
        def _():
            for cp in copies(nxt_ref[t], next_c):
                cp.start()


def _grouped_kernel(te_ref, first_ref, run_ref, nxt_ref, cnt_ref, valid_ref, x_ref, *rest, n_w, sub, math):
    w_hbm, o_ref = rest[:n_w], rest[n_w]
    stage_ref, wb_ref, sems = rest[n_w + 1:]
    t = pl.program_id(1)
    tm, tc = o_ref.shape

    def copies(e, c):
        cols = pl.ds(pl.multiple_of(c * tc, tc), tc)
        return [pltpu.make_async_copy(w_hbm[k].at[e, :, cols], stage_ref.at[k], sems.at[k])
                for k in range(n_w)]

    def cast():
        for k in range(n_w):
            wb_ref[k] = stage_ref[k].astype(BF16)

    _stream_run_weights((te_ref, first_ref, run_ref, nxt_ref, cnt_ref), copies, cast)

    for v in range(tm // sub + 1):
        rows = v * sub

        @pl.when(valid_ref[t] == v)
        def _():
            if rows:
                o_ref[:rows, :] = math(x_ref[:rows, :], *[wb_ref[k] for k in range(n_w)])
            if rows < tm:
                o_ref[rows:, :] = jnp.zeros((tm - rows, tc), o_ref.dtype)


def grouped_matmul(x, ws, tile_expert, used, valid, math, out_dtype, *, tm, sub, tc):
    N, K = x.shape
    E, _, C = ws[0].shape
    n_w = len(ws)
    assert N % tm == 0 and C % tc == 0 and tm % sub == 0
    sched = (tile_expert,) + _run_schedule(tile_expert, used) + (valid,)
    return pl.pallas_call(
        functools.partial(_grouped_kernel, n_w=n_w, sub=sub, math=math),
        grid_spec=pltpu.PrefetchScalarGridSpec(
            num_scalar_prefetch=len(sched),
            grid=(C // tc, N // tm),
            in_specs=[pl.BlockSpec((tm, K), lambda c, t, *_: (t, 0))] + [pl.BlockSpec(memory_space=pl.ANY)] * n_w,
            out_specs=pl.BlockSpec((tm, tc), lambda c, t, *_: (t, c)),
            scratch_shapes=[pltpu.VMEM((n_w, K, tc), F32), pltpu.VMEM((n_w, K, tc), BF16),
                            pltpu.SemaphoreType.DMA((n_w,))]),
        out_shape=jax.ShapeDtypeStruct((N, C), out_dtype),
        compiler_params=_params(2),
        name="grouped_matmul",
    )(*sched, x, *ws)


def _down_math(x, w):
    return jnp.dot(x, w, preferred_element_type=F32)


def _combine_kernel(pos_ref, y_ref, h_ref, gate_ref, g_ref, b_ref, o_ref, stage_ref, sems, *, tc):
    i = pl.program_id(0)
    n = pl.num_programs(0)
    slot = i % 2

    def row_copy(y_row, r, k, slot):
        return pltpu.make_async_copy(y_ref.at[pl.ds(y_row, 1), :],
                                     stage_ref.at[slot, k, pl.ds(r, 1), :], sems.at[slot])

    def issue(step, slot):
        def start_row(r, c):
            for k in range(TOP_K):
                row_copy(pos_ref[(step * tc + r) * TOP_K + k], r, k, slot).start()
            return c
        lax.fori_loop(0, tc, start_row, 0, unroll=DMA_UNROLL)

    @pl.when(i == 0)
    def _():
        issue(0, 0)

    @pl.when(i + 1 < n)
    def _():
        issue(i + 1, 1 - slot)

    def wait_row(r, c):
        for k in range(TOP_K):
            row_copy(0, r, k, slot).wait()
        return c
    lax.fori_loop(0, tc, wait_row, 0, unroll=DMA_UNROLL)

    for r in range(0, tc, EPI_ROWS):
        rs = slice(r, r + EPI_ROWS)
        gates = gate_ref[rs, :]
        moe = gates[:, 0:1] * stage_ref[slot, 0, rs, :]
        for k in range(1, TOP_K):
            moe = moe + gates[:, k:k + 1] * stage_ref[slot, k, rs, :]
        o_ref[rs, :] = _layer_norm(ALPHA * h_ref[rs, :] + moe, g_ref[...], b_ref[...])


def combine_ln(y, pos, h, gates, g, b, *, tc=256):
    S, D = h.shape
    assert S % tc == 0
    row = pl.BlockSpec((tc, D), lambda i, pos: (i, 0))
    vec = pl.BlockSpec((1, D), lambda i, pos: (0, 0))
    return pl.pallas_call(
        functools.partial(_combine_kernel, tc=tc),
        grid_spec=pltpu.PrefetchScalarGridSpec(
            num_scalar_prefetch=1,
            grid=(S // tc,),
            in_specs=[pl.BlockSpec(memory_space=pl.ANY), row,
                      pl.BlockSpec((tc, TOP_K), lambda i, pos: (i, 0)), vec, vec],
            out_specs=row,
            scratch_shapes=[pltpu.VMEM((2, TOP_K, tc, D), F32), pltpu.SemaphoreType.DMA((2,))]),
        out_shape=jax.ShapeDtypeStruct((S, D), F32),
        compiler_params=_params(1),
        name="combine_ln",
    )(pos.reshape(-1), y, h, gates, g.reshape(1, D), b.reshape(1, D))


def _routing_tables(idx, n_experts, tile, sub):
    S, K = idx.shape
    flat = idx.reshape(-1)
    onehot = (flat[:, None] == jnp.arange(n_experts, dtype=jnp.int32)[None, :]).astype(jnp.int32)
    csum = jnp.cumsum(onehot, axis=0)
    rank = jnp.sum(onehot * csum, axis=1) - 1
    counts = csum[-1]
    padded = (counts + tile - 1) // tile * tile
    ends = jnp.cumsum(padded)
    starts = ends - padded
    pos = jnp.sum(onehot * starts[None, :], axis=1) + rank
    n_rows = S * K + n_experts * tile
    src = jnp.zeros((n_rows,), jnp.int32).at[pos].set(jnp.arange(S * K, dtype=jnp.int32) // K)
    used = ends[-1] // tile
    t = jnp.arange(n_rows // tile, dtype=jnp.int32)
    tile_expert = jnp.sum((jnp.minimum(t, used - 1)[:, None] * tile >= ends[None, :]).astype(jnp.int32), axis=1)
    rows_in_tile = jnp.clip(counts[tile_expert] - (t * tile - starts[tile_expert]), 0, tile)
    valid = jnp.where(t < used, (rows_in_tile + sub - 1) // sub, 0)
    per = tile // sub
    sub_live = (jnp.arange(n_rows // sub, dtype=jnp.int32) % per) < jnp.repeat(valid, per)
    i32 = lambda a: a.astype(jnp.int32)
    return pos.reshape(S, K), src, i32(tile_expert), i32(used.reshape(1)), i32(valid), i32(sub_live)


def moe_ln(h, w_router, wg, wu, wd, g, b):
    E = w_router.shape[1]
    tm, sub = EXPERT_TILE_ROWS, EXPERT_SUB_ROWS
    idx, gates = route_top2(h, w_router)
    pos, src, tile_expert, used, valid, sub_live = _routing_tables(idx, E, tm, sub)
    xg = gather_rows_bf16(h, src, sub_live, tg=sub)
    hid = grouped_matmul(xg, (wg, wu), tile_expert, used, valid, _glu_math, BF16,
                         tm=tm, sub=sub, tc=EXPERT_FF_TILE)
    y = grouped_matmul(hid, (wd,), tile_expert, used, valid, _down_math, F32,
                       tm=tm, sub=sub, tc=EXPERT_OUT_TILE)
    return combine_ln(y, pos, h, gates, g, b)


def kernel(x, l0_pool_w_in, l0_pool_w_group, l0_pool_scale, l0_ln1_g, l0_ln1_b, l0_ffn_w_gate, l0_ffn_w_up, l0_ffn_w_down, l0_ln2_g, l0_ln2_b, l1_attn_w_qkv, l1_attn_w_o, l1_ln1_g, l1_ln1_b, l1_moe_w_router, l1_moe_w_gate, l1_moe_w_up, l1_moe_w_down, l1_ln2_g, l1_ln2_b):
    B, S, D = x.shape
    x0 = x.reshape(B * S, D)
    assert B == 1, "sequence mixing below treats all rows as one sequence"
    bf = lambda w: w.astype(BF16)

    u = matmul(bf(x0), l0_pool_w_in, F32)
    x1, x1b = pool_mix_ln(u, x0, bf(l0_pool_w_group), l0_pool_scale, l0_ln1_g, l0_ln1_b)
    n_tiles = (B * S) // DENSE_ROWS
    hid = grouped_matmul(x1b, (l0_ffn_w_gate[None], l0_ffn_w_up[None]), jnp.zeros((n_tiles,), jnp.int32),
                         jnp.full((1,), n_tiles, jnp.int32), jnp.ones((n_tiles,), jnp.int32), _glu_math, BF16,
                         tm=DENSE_ROWS, sub=DENSE_ROWS, tc=DENSE_FF_TILE)
    x2, x2b = matmul_residual_ln(hid, bf(l0_ffn_w_down), x1, l0_ln2_g, l0_ln2_b, emit_bf16=True)

    qkv = matmul(x2b, l1_attn_w_qkv, BF16, scaled_cols=D, scale=HEAD_DIM ** -0.5)
    o = stick_breaking_attention(qkv)
    (x3,) = matmul_residual_ln(o, l1_attn_w_o, x2, l1_ln1_g, l1_ln1_b, emit_bf16=False)
    x4 = moe_ln(x3, l1_moe_w_router, l1_moe_w_gate, l1_moe_w_up, l1_moe_w_down, l1_ln2_g, l1_ln2_b)
    return x4.reshape(B, S, D)
```

```python
import functools

import jax
import jax.numpy as jnp
from jax import lax
from jax.experimental import pallas as pl
from jax.experimental.pallas import tpu as pltpu

HEAD_DIM = 128
POOL_WINDOWS = (2, 4, 8, 16)
POOL_HALO = 16
TOP_K = 2
LN_EPS = 1e-5
DEPTH = 2
ALPHA = (2 * DEPTH) ** 0.25
LANES = 128
DENSE_ROWS = 1024
DENSE_FF_TILE = 256
EXPERT_FF_TILE = 256
EXPERT_OUT_TILE = 512
EXPERT_TILE_ROWS = 1024
EXPERT_SUB_ROWS = 256
DMA_UNROLL = 8
EPI_COLS = 1024
EPI_ROWS = 128
VMEM_LIMIT = 56 * 1024 * 1024
EXP_UNDERFLOW = -104.0
MASKED_LOGIT = -1e30

BF16 = jnp.bfloat16
F32 = jnp.float32


def _params(n_axes):
    return pltpu.CompilerParams(dimension_semantics=("arbitrary",) * n_axes,
                                vmem_limit_bytes=VMEM_LIMIT)


def _layer_norm(t, g, b):
    mu = jnp.mean(t, axis=-1, keepdims=True)
    c = t - mu
    var = jnp.mean(c * c, axis=-1, keepdims=True)
    return c * lax.rsqrt(var + LN_EPS) * g + b


def _mm_kernel(x_ref, w_ref, o_ref, wb_ref, *, scaled_tiles, scale):
    @pl.when(pl.program_id(1) == 0)
    def _():
        wb_ref[...] = w_ref[...].astype(BF16)

    acc = jnp.dot(x_ref[...], wb_ref[...], preferred_element_type=F32)
    if scaled_tiles:
        acc = acc * jnp.where(pl.program_id(0) < scaled_tiles, scale, 1.0)
    o_ref[...] = acc.astype(o_ref.dtype)


def matmul(x, w, out_dtype, *, tm=1024, tn=512, scaled_cols=0, scale=1.0):
    M, K = x.shape
    N = w.shape[1]
    tm, tn = min(tm, M), min(tn, N)
    assert M % tm == 0 and N % tn == 0 and scaled_cols % tn == 0
    return pl.pallas_call(
        functools.partial(_mm_kernel, scaled_tiles=scaled_cols // tn, scale=scale),
        grid=(N // tn, M // tm),
        in_specs=[pl.BlockSpec((tm, K), lambda j, i: (i, 0)),
                  pl.BlockSpec((K, tn), lambda j, i: (0, j))],
        out_specs=pl.BlockSpec((tm, tn), lambda j, i: (i, j)),
        out_shape=jax.ShapeDtypeStruct((M, N), out_dtype),
        scratch_shapes=[pltpu.VMEM((K, tn), BF16)],
        compiler_params=_params(2),
        name="matmul",
    )(x, w)


def _mm_ln_kernel(x_ref, w_ref, res_hbm, g_ref, b_ref, o_ref, *rest, nk, k_tail):
    obf_ref = rest[0] if len(rest) == 3 else None
    res_ref, res_sem = rest[-2:]
    i, k = pl.program_id(0), pl.program_id(1)
    tm, D = o_ref.shape
    tk = x_ref.shape[1]
    res_copy = pltpu.make_async_copy(res_hbm.at[pl.ds(pl.multiple_of(i * tm, tm), tm), :], res_ref, res_sem)

    @pl.when(k == 0)
    def _():
        res_copy.start()
        o_ref[...] = jnp.zeros_like(o_ref)

    def accumulate(valid):
        x = x_ref[...]
        if valid < tk:
            keep_x = lax.broadcasted_iota(jnp.int32, (1, tk), 1) < valid
            keep_w = lax.broadcasted_iota(jnp.int32, (tk, 1), 0) < valid
            x = jnp.where(keep_x, x.astype(F32), 0.0).astype(BF16)
        for c in range(0, D, EPI_COLS):
            cs = slice(c, c + EPI_COLS)
            w = w_ref[:, cs]
            if valid < tk:
                w = jnp.where(keep_w, w.astype(F32), 0.0)
            o_ref[:, cs] += jnp.dot(x, w.astype(BF16), preferred_element_type=F32)

    if k_tail == tk:
        accumulate(tk)
    else:
        pl.when(k < nk - 1)(lambda: accumulate(tk))
        pl.when(k == nk - 1)(lambda: accumulate(k_tail))

    @pl.when(k == nk - 1)
    def _():
        res_copy.wait()
        for r in range(0, tm, EPI_ROWS):
            rs = slice(r, r + EPI_ROWS)
            y = _layer_norm(ALPHA * res_ref[rs, :] + o_ref[rs, :], g_ref[...], b_ref[...])
            o_ref[rs, :] = y
            if obf_ref is not None:
                obf_ref[rs, :] = y.astype(BF16)


def matmul_residual_ln(x, w, res, g, b, *, emit_bf16, tm=512, tk=512):
    M, K = x.shape
    D = w.shape[1]
    tm = min(tm, M)
    assert M % tm == 0
    nk = pl.cdiv(K, tk)
    row = pl.BlockSpec((tm, D), lambda i, k: (i, 0))
    vec = pl.BlockSpec((1, D), lambda i, k: (0, 0))
    return pl.pallas_call(
        functools.partial(_mm_ln_kernel, nk=nk, k_tail=K - (nk - 1) * tk),
        grid=(M // tm, nk),
        in_specs=[pl.BlockSpec((tm, tk), lambda i, k: (i, k)),
                  pl.BlockSpec((tk, D), lambda i, k: (k, 0)),
                  pl.BlockSpec(memory_space=pl.ANY),
                  vec, vec],
        out_specs=[row, row] if emit_bf16 else [row],
        out_shape=[jax.ShapeDtypeStruct((M, D), F32)] + [jax.ShapeDtypeStruct((M, D), BF16)] * emit_bf16,
        scratch_shapes=[pltpu.VMEM((tm, D), F32), pltpu.SemaphoreType.DMA(())],
        compiler_params=_params(2),
        name="matmul_residual_ln",
    )(x, w, res, g.reshape(1, D), b.reshape(1, D))


def _pool_kernel(u_ref, halo_ref, x_ref, wg_ref, sc_ref, g_ref, b_ref, o_ref, obf_ref, ext_ref, *, tm, C):
    i = pl.program_id(0)
    pos = i * tm + 1 + lax.broadcasted_iota(jnp.int32, (tm, 1), 0)
    for gi, w in enumerate(POOL_WINDOWS):
        cs = slice(gi * C, (gi + 1) * C)
        ext_ref[0:POOL_HALO, :] = jnp.where(i > 0, halo_ref[:, cs], 0.0)
        ext_ref[POOL_HALO:, :] = u_ref[:, cs]
        u = u_ref[:, cs]
        win = u
        for k in range(1, w):
            win = win + ext_ref[pl.ds(POOL_HALO - k, tm), :]
        count = jnp.minimum(pos, w).astype(F32)
        d = win / count - u
        y = jnp.dot(d.astype(BF16), wg_ref[gi], preferred_element_type=F32)
        o_ref[:, cs] = ALPHA * x_ref[:, cs] + y * sc_ref[:, cs]
    for r in range(0, tm, EPI_ROWS):
        rs = slice(r, r + EPI_ROWS)
        y = _layer_norm(o_ref[rs, :], g_ref[...], b_ref[...])
        o_ref[rs, :] = y
        obf_ref[rs, :] = y.astype(BF16)


def pool_mix_ln(u, x, w_group, scale, g, b, *, tm=256):
    S, D = u.shape
    G, C, _ = w_group.shape
    assert G == len(POOL_WINDOWS) and G * C == D and S % tm == 0 and tm % POOL_HALO == 0
    hb = tm // POOL_HALO
    row = pl.BlockSpec((tm, D), lambda i: (i, 0))
    vec = pl.BlockSpec((1, D), lambda i: (0, 0))
    return pl.pallas_call(
        functools.partial(_pool_kernel, tm=tm, C=C),
        grid=(S // tm,),
        in_specs=[row,
                  pl.BlockSpec((POOL_HALO, D), lambda i: (jnp.maximum(i * hb - 1, 0), 0)),
                  row,
                  pl.BlockSpec((G, C, C), lambda i: (0, 0, 0)),
                  vec, vec, vec],
        out_specs=[row, row],
        out_shape=[jax.ShapeDtypeStruct((S, D), F32), jax.ShapeDtypeStruct((S, D), BF16)],
        scratch_shapes=[pltpu.VMEM((tm + POOL_HALO, C), F32)],
        compiler_params=_params(1),
        name="pool_mix_ln",
    )(u, u, x, w_group, scale.reshape(1, D), g.reshape(1, D), b.reshape(1, D))


def _glu_math(x, wg, wu):
    gate = jnp.dot(x, wg, preferred_element_type=F32)
    up = jnp.dot(x, wu, preferred_element_type=F32)
    return (gate * jax.nn.sigmoid(gate) * up).astype(BF16)


def _sb_tile(q, ks, vs, tri2, drop_run, mask):
    tb = tri2.shape[1]
    z = lax.dot_general(q, ks, (((1,), (1,)), ((), ())), preferred_element_type=F32)
    if mask is not None:
        z = jnp.where(mask, z, MASKED_LOGIT)
    sp = jnp.log(1.0 + jnp.exp(-jnp.abs(z)))
    log_beta = jnp.minimum(z, 0.0) - sp
    drop = jnp.maximum(z, 0.0) + sp
    later = [None] * (z.shape[1] // tb)
    for blk in reversed(range(len(later))):
        d = drop[:, blk * tb:(blk + 1) * tb]
        hi = d.astype(BF16)
        lo = (d - hi.astype(F32)).astype(BF16)
        later[blk] = jnp.dot(jnp.concatenate([hi, lo], axis=1), tri2, preferred_element_type=F32) + drop_run
        drop_run = drop_run + jnp.sum(d, axis=1, keepdims=True)
    w = jnp.exp(log_beta - jnp.concatenate(later, axis=1))
    pv = jnp.dot(w.astype(BF16), vs, preferred_element_type=F32)
    return pv, drop_run


def _attn_kernel(q_ref, k_ref, v_ref, tri_ref, o_ref, *, tq, heads):
    i = pl.program_id(1)
    t0 = i * tq
    w0 = pl.multiple_of(jnp.maximum(t0 - tq, 0), tq)
    t_pos = t0 + lax.broadcasted_iota(jnp.int32, (tq, 1), 0)
    mask = (w0 + lax.broadcasted_iota(jnp.int32, (1, 2 * tq), 1)) < t_pos
    zero = jnp.zeros((tq, 1), F32)
    accs, runs = [], []
    for g in range(heads):
        hs = slice(g * HEAD_DIM, (g + 1) * HEAD_DIM)
        pv, run = _sb_tile(q_ref[:, hs], k_ref[pl.ds(w0, 2 * tq), hs], v_ref[pl.ds(w0, 2 * tq), hs],
                           tri_ref[...], zero, mask)
        accs.append(pv)
        runs.append(run)

    def alive(runs):
        return functools.reduce(jnp.minimum, [jnp.min(r) for r in runs]) < -EXP_UNDERFLOW

    def cond(carry):
        j, runs, _ = carry
        return jnp.logical_and(j >= 0, alive(runs))

    def body(carry):
        j, runs, accs = carry
        start = pl.multiple_of(j * tq, tq)
        new_runs, new_accs = [], []
        for g in range(heads):
            hs = slice(g * HEAD_DIM, (g + 1) * HEAD_DIM)
            pv, run = _sb_tile(q_ref[:, hs], k_ref[pl.ds(start, tq), hs], v_ref[pl.ds(start, tq), hs],
                               tri_ref[...], runs[g], None)
            new_accs.append(accs[g] + pv)
            new_runs.append(run)
        return j - 1, tuple(new_runs), tuple(new_accs)

    _, _, accs = lax.while_loop(cond, body, (w0 // tq - 1, tuple(runs), tuple(accs)))
    for g in range(heads):
        o_ref[:, g * HEAD_DIM:(g + 1) * HEAD_DIM] = accs[g].astype(o_ref.dtype)


def _stacked_tri(n):
    r = lax.broadcasted_iota(jnp.int32, (n, n), 0)
    c = lax.broadcasted_iota(jnp.int32, (n, n), 1)
    tri = (r > c).astype(BF16)
    return jnp.concatenate([tri, tri], axis=0)


def stick_breaking_attention(qkv, *, tq=256, heads=4):
    S, D3 = qkv.shape
    D = D3 // 3
    H = D // HEAD_DIM
    assert S % tq == 0 and S >= 2 * tq and H % heads == 0
    hw = heads * HEAD_DIM
    nb = D // hw
    return pl.pallas_call(
        functools.partial(_attn_kernel, tq=tq, heads=heads),
        grid=(nb, S // tq),
        in_specs=[pl.BlockSpec((tq, hw), lambda h, i: (i, h)),
                  pl.BlockSpec((S, hw), lambda h, i: (0, nb + h)),
                  pl.BlockSpec((S, hw), lambda h, i: (0, 2 * nb + h)),
                  pl.BlockSpec((2 * tq, tq), lambda h, i: (0, 0))],
        out_specs=pl.BlockSpec((tq, hw), lambda h, i: (i, h)),
        out_shape=jax.ShapeDtypeStruct((S, D), BF16),
        compiler_params=_params(2),
        name="stick_breaking_attention",
    )(qkv, qkv, qkv, _stacked_tri(tq))


def _router_kernel(h_ref, w_ref, o_ref, *, n_experts):
    h = h_ref[...]
    w = w_ref[...]
    h1 = h.astype(BF16)
    h2 = (h - h1.astype(F32)).astype(BF16)
    h3 = (h - h1.astype(F32) - h2.astype(F32)).astype(BF16)
    w1 = w.astype(BF16)
    w2 = (w - w1.astype(F32)).astype(BF16)
    w3 = (w - w1.astype(F32) - w2.astype(F32)).astype(BF16)
    dot = functools.partial(jnp.dot, preferred_element_type=F32)
    logits = (dot(h3, w1) + dot(h2, w2) + dot(h1, w3)) + (dot(h2, w1) + dot(h1, w2)) + dot(h1, w1)
    lane = lax.broadcasted_iota(jnp.int32, logits.shape, 1)
    lane_f = lane.astype(F32)
    neg = jnp.float32(-jnp.inf)
    logits = jnp.where(lane < n_experts, logits, neg)
    m1 = jnp.max(logits, axis=1, keepdims=True)
    i1 = jnp.min(jnp.where(logits == m1, lane_f, float(LANES)), axis=1, keepdims=True)
    rest = jnp.where(lane_f == i1, neg, logits)
    m2 = jnp.max(rest, axis=1, keepdims=True)
    i2 = jnp.min(jnp.where(rest == m2, lane_f, float(LANES)), axis=1, keepdims=True)
    e2 = jnp.exp(m2 - m1)
    g1 = 1.0 / (1.0 + e2)
    g2 = e2 / (1.0 + e2)
    out = jnp.where(lane == 0, i1, 0.0)
    out = jnp.where(lane == 1, i2, out)
    out = jnp.where(lane == 2, g1, out)
    out = jnp.where(lane == 3, g2, out)
    o_ref[...] = out


def route_top2(h, w_router, *, tm=256):
    S, D = h.shape
    E = w_router.shape[1]
    assert E <= LANES and S % tm == 0
    w_pad = jnp.pad(w_router, ((0, 0), (0, LANES - E)))
    info = pl.pallas_call(
        functools.partial(_router_kernel, n_experts=E),
        grid=(S // tm,),
        in_specs=[pl.BlockSpec((tm, D), lambda i: (i, 0)),
                  pl.BlockSpec((D, LANES), lambda i: (0, 0))],
        out_specs=pl.BlockSpec((tm, LANES), lambda i: (i, 0)),
        out_shape=jax.ShapeDtypeStruct((S, LANES), F32),
        compiler_params=_params(1),
        name="route_top2",
    )(h, w_pad)
    return info[:, 0:TOP_K].astype(jnp.int32), info[:, TOP_K:2 * TOP_K]


def _gather_kernel(src_ref, live_ref, h_ref, o_ref, stage_ref, sems, *, tg):
    i, n = pl.program_id(0), pl.num_programs(0)
    slot = i % 2

    def row_copy(src_row, r, slot):
        return pltpu.make_async_copy(h_ref.at[pl.ds(src_row, 1), :],
                                     stage_ref.at[slot, pl.ds(r, 1), :], sems.at[slot])

    def issue(step, slot):
        def start_row(r, c):
            row_copy(src_ref[step * tg + r], r, slot).start()
            return c
        lax.fori_loop(0, tg, start_row, 0, unroll=DMA_UNROLL)

    @pl.when(jnp.logical_and(i == 0, live_ref[0] == 1))
    def _():
        issue(0, 0)

    @pl.when(jnp.logical_and(i + 1 < n, live_ref[jnp.minimum(i + 1, n - 1)] == 1))
    def _():
        issue(i + 1, 1 - slot)

    @pl.when(live_ref[i] == 1)
    def _():
        def wait_row(r, c):
            row_copy(0, r, slot).wait()
            return c
        lax.fori_loop(0, tg, wait_row, 0, unroll=DMA_UNROLL)
        o_ref[...] = stage_ref[slot].astype(BF16)

    @pl.when(live_ref[i] == 0)
    def _():
        o_ref[...] = jnp.zeros_like(o_ref)


def gather_rows_bf16(h, src_rows, tile_live, *, tg):
    S, D = h.shape
    N = src_rows.shape[0]
    assert N % tg == 0 and tile_live.shape[0] == N // tg
    return pl.pallas_call(
        functools.partial(_gather_kernel, tg=tg),
        grid_spec=pltpu.PrefetchScalarGridSpec(
            num_scalar_prefetch=2,
            grid=(N // tg,),
            in_specs=[pl.BlockSpec(memory_space=pl.ANY)],
            out_specs=pl.BlockSpec((tg, D), lambda i, src, live: (i, 0)),
            scratch_shapes=[pltpu.VMEM((2, tg, D), F32), pltpu.SemaphoreType.DMA((2,))]),
        out_shape=jax.ShapeDtypeStruct((N, D), BF16),
        compiler_params=_params(1),
        name="gather_rows_bf16",
    )(src_rows, tile_live, h)


def _run_schedule(tile_expert, used):
    T = tile_expert.shape[0]
    t = jnp.arange(T, dtype=jnp.int32)
    prev = jnp.concatenate([tile_expert[:1], tile_expert[:-1]])
    first = jnp.logical_and(jnp.logical_or(t == 0, tile_expert != prev), t < used[0])
    run = jnp.cumsum(first.astype(jnp.int32)) - 1
    later_first = jnp.logical_and(first[None, :], t[None, :] > t[:, None])
    nxt_tile = jnp.min(jnp.where(later_first, t[None, :], T), axis=1)
    nxt = jnp.where(nxt_tile < T, tile_expert[jnp.minimum(nxt_tile, T - 1)], tile_expert[0])
    counts = jnp.stack([used[0], jnp.sum(first.astype(jnp.int32))])
    return first.astype(jnp.int32), run.astype(jnp.int32), nxt.astype(jnp.int32), counts.astype(jnp.int32)


def _stream_run_weights(sched, copies, on_arrival):
    te_ref, first_ref, run_ref, nxt_ref, cnt_ref = sched
    c, t, nc = pl.program_id(0), pl.program_id(1), pl.num_programs(0)
    n_runs = cnt_ref[1]

    @pl.when(first_ref[t] == 1)
    def _():
        @pl.when(jnp.logical_and(c == 0, run_ref[t] == 0))
        def _():
            for cp in copies(te_ref[t], c):
                cp.start()

        for cp in copies(te_ref[t], c):
            cp.wait()
        on_arrival()
        next_c = c + (run_ref[t] == n_runs - 1).astype(jnp.int32)

        @pl.when(next_c < nc)
        def _():
            for cp in copies(nxt_ref[t], next_c):
                cp.start()


def _grouped_kernel(te_ref, first_ref, run_ref, nxt_ref, cnt_ref, valid_ref, x_ref, *rest, n_w, sub, math):
    w_hbm, o_ref = rest[:n_w], rest[n_w]
    stage_ref, wb_ref, sems = rest[n_w + 1:]
    t = pl.program_id(1)
    tm, tc = o_ref.shape

    def copies(e, c):
        cols = pl.ds(pl.multiple_of(c * tc, tc), tc)
        return [pltpu.make_async_copy(w_hbm[k].at[e, :, cols], stage_ref.at[k], sems.at[k])
                for k in range(n_w)]

    def cast():
        for k in range(n_w):
            wb_ref[k] = stage_ref[k].astype(BF16)

    _stream_run_weights((te_ref, first_ref, run_ref, nxt_ref, cnt_ref), copies, cast)

    for v in range(tm // sub + 1):
        rows = v * sub

        @pl.when(valid_ref[t] == v)
        def _():
            if rows:
                o_ref[:rows, :] = math(x_ref[:rows, :], *[wb_ref[k] for k in range(n_w)])
            if rows < tm:
                o_ref[rows:, :] = jnp.zeros((tm - rows, tc), o_ref.dtype)


def grouped_matmul(x, ws, tile_expert, used, valid, math, out_dtype, *, tm, sub, tc):
    N, K = x.shape
    E, _, C = ws[0].shape
    n_w = len(ws)
    assert N % tm == 0 and C % tc == 0 and tm % sub == 0
    sched = (tile_expert,) + _run_schedule(tile_expert, used) + (valid,)
    return pl.pallas_call(
        functools.partial(_grouped_kernel, n_w=n_w, sub=sub, math=math),
        grid_spec=pltpu.PrefetchScalarGridSpec(
            num_scalar_prefetch=len(sched),
            grid=(C // tc, N // tm),
            in_specs=[pl.BlockSpec((tm, K), lambda c, t, te, first, run, nxt, cnt, valid:
                                   (jnp.minimum(t, cnt[0] - 1), 0))]
                     + [pl.BlockSpec(memory_space=pl.ANY)] * n_w,
            out_specs=pl.BlockSpec((tm, tc), lambda c, t, *_: (t, c)),
            scratch_shapes=[pltpu.VMEM((n_w, K, tc), F32), pltpu.VMEM((n_w, K, tc), BF16),
                            pltpu.SemaphoreType.DMA((n_w,))]),
        out_shape=jax.ShapeDtypeStruct((N, C), out_dtype),
        compiler_params=_params(2),
        name="grouped_matmul",
    )(*sched, x, *ws)


def _down_math(x, w):
    return jnp.dot(x, w, preferred_element_type=F32)


def _combine_kernel(pos_ref, y_ref, h_ref, gate_ref, g_ref, b_ref, o_ref, stage_ref, sems, *, tc):
    i = pl.program_id(0)
    n = pl.num_programs(0)
    slot = i % 2

    def row_copy(y_row, r, k, slot):
        return pltpu.make_async_copy(y_ref.at[pl.ds(y_row, 1), :],
                                     stage_ref.at[slot, k, pl.ds(r, 1), :], sems.at[slot])

    def issue(step, slot):
        def start_row(r, c):
            for k in range(TOP_K):
                row_copy(pos_ref[(step * tc + r) * TOP_K + k], r, k, slot).start()
            return c
        lax.fori_loop(0, tc, start_row, 0, unroll=DMA_UNROLL)

    @pl.when(i == 0)
    def _():
        issue(0, 0)

    @pl.when(i + 1 < n)
    def _():
        issue(i + 1, 1 - slot)

    def wait_row(r, c):
        for k in range(TOP_K):
            row_copy(0, r, k, slot).wait()
        return c
    lax.fori_loop(0, tc, wait_row, 0, unroll=DMA_UNROLL)

    for r in range(0, tc, EPI_ROWS):
        rs = slice(r, r + EPI_ROWS)
        gates = gate_ref[rs, :]
        moe = gates[:, 0:1] * stage_ref[slot, 0, rs, :]
        for k in range(1, TOP_K):
            moe = moe + gates[:, k:k + 1] * stage_ref[slot, k, rs, :]
        o_ref[rs, :] = _layer_norm(ALPHA * h_ref[rs, :] + moe, g_ref[...], b_ref[...])


def combine_ln(y, pos, h, gates, g, b, *, tc=256):
    S, D = h.shape
    assert S % tc == 0
    row = pl.BlockSpec((tc, D), lambda i, pos: (i, 0))
    vec = pl.BlockSpec((1, D), lambda i, pos: (0, 0))
    return pl.pallas_call(
        functools.partial(_combine_kernel, tc=tc),
        grid_spec=pltpu.PrefetchScalarGridSpec(
            num_scalar_prefetch=1,
            grid=(S // tc,),
            in_specs=[pl.BlockSpec(memory_space=pl.ANY), row,
                      pl.BlockSpec((tc, TOP_K), lambda i, pos: (i, 0)), vec, vec],
            out_specs=row,
            scratch_shapes=[pltpu.VMEM((2, TOP_K, tc, D), F32), pltpu.SemaphoreType.DMA((2,))]),
        out_shape=jax.ShapeDtypeStruct((S, D), F32),
        compiler_params=_params(1),
        name="combine_ln",
    )(pos.reshape(-1), y, h, gates, g.reshape(1, D), b.reshape(1, D))


def _routing_tables(idx, n_experts, tile, sub):
    S, K = idx.shape
    flat = idx.reshape(-1)
    onehot = (flat[:, None] == jnp.arange(n_experts, dtype=jnp.int32)[None, :]).astype(jnp.int32)
    csum = jnp.cumsum(onehot, axis=0)
    rank = jnp.sum(onehot * csum, axis=1) - 1
    counts = csum[-1]
    padded = (counts + tile - 1) // tile * tile
    ends = jnp.cumsum(padded)
    starts = ends - padded
    pos = jnp.sum(onehot * starts[None, :], axis=1) + rank
    n_rows = S * K + n_experts * tile
    src = jnp.zeros((n_rows,), jnp.int32).at[pos].set(jnp.arange(S * K, dtype=jnp.int32) // K)
    used = ends[-1] // tile
    t = jnp.arange(n_rows // tile, dtype=jnp.int32)
    tile_expert = jnp.sum((jnp.minimum(t, used - 1)[:, None] * tile >= ends[None, :]).astype(jnp.int32), axis=1)
    rows_in_tile = jnp.clip(counts[tile_expert] - (t * tile - starts[tile_expert]), 0, tile)
    valid = jnp.where(t < used, (rows_in_tile + sub - 1) // sub, 0)
    per = tile // sub
    sub_live = (jnp.arange(n_rows // sub, dtype=jnp.int32) % per) < jnp.repeat(valid, per)
    i32 = lambda a: a.astype(jnp.int32)
    return pos.reshape(S, K), src, i32(tile_expert), i32(used.reshape(1)), i32(valid), i32(sub_live)


def moe_ln(h, w_router, wg, wu, wd, g, b):
    E = w_router.shape[1]
    tm, sub = EXPERT_TILE_ROWS, EXPERT_SUB_ROWS
    idx, gates = route_top2(h, w_router)
    pos, src, tile_expert, used, valid, sub_live = _routing_tables(idx, E, tm, sub)
    xg = gather_rows_bf16(h, src, sub_live, tg=sub)
    hid = grouped_matmul(xg, (wg, wu), tile_expert, used, valid, _glu_math, BF16,
                         tm=tm, sub=sub, tc=EXPERT_FF_TILE)
    y = grouped_matmul(hid, (wd,), tile_expert, used, valid, _down_math, F32,
                       tm=tm, sub=sub, tc=EXPERT_OUT_TILE)
    return combine_ln(y, pos, h, gates, g, b)


def kernel(x, l0_pool_w_in, l0_pool_w_group, l0_pool_scale, l0_ln1_g, l0_ln1_b, l0_ffn_w_gate, l0_ffn_w_up, l0_ffn_w_down, l0_ln2_g, l0_ln2_b, l1_attn_w_qkv, l1_attn_w_o, l1_ln1_g, l1_ln1_b, l1_moe_w_router, l1_moe_w_gate, l1_moe_w_up, l1_moe_w_down, l1_ln2_g, l1_ln2_b):
    B, S, D = x.shape
    x0 = x.reshape(B * S, D)
    assert B == 1, "sequence mixing below treats all rows as one sequence"
    bf = lambda w: w.astype(BF16)

    u = matmul(bf(x0), l0_pool_w_in, F32)
    x1, x1b = pool_mix_ln(u, x0, bf(l0_pool_w_group), l0_pool_scale, l0_ln1_g, l0_ln1_b)
    n_tiles = (B * S) // DENSE_ROWS
    hid = grouped_matmul(x1b, (l0_ffn_w_gate[None], l0_ffn_w_up[None]), jnp.zeros((n_tiles,), jnp.int32),
                         jnp.full((1,), n_tiles, jnp.int32), jnp.ones((n_tiles,), jnp.int32), _glu_math, BF16,
                         tm=DENSE_ROWS, sub=DENSE_ROWS, tc=DENSE_FF_TILE)
    x2, x2b = matmul_residual_ln(hid, bf(l0_ffn_w_down), x1, l0_ln2_g, l0_ln2_b, emit_bf16=True)

    qkv = matmul(x2b, l1_attn_w_qkv, BF16, scaled_cols=D, scale=HEAD_DIM ** -0.5)
    o = stick_breaking_attention(qkv)
    (x3,) = matmul_residual_ln(o, l1_attn_w_o, x2, l1_ln1_g, l1_ln1_b, emit_bf16=False)
    x4 = moe_ln(x3, l1_moe_w_router, l1_moe_w_gate, l1_moe_w_up, l1_moe_w_down, l1_ln2_g, l1_ln2_b)
    return x4.reshape(B, S, D)
```

```python
import functools

import jax
import jax.numpy as jnp
from jax import lax
from jax.experimental import pallas as pl
from jax.experimental.pallas import tpu as pltpu

HEAD_DIM = 128
POOL_WINDOWS = (2, 4, 8, 16)
POOL_HALO = 16
TOP_K = 2
LN_EPS = 1e-5
DEPTH = 2
ALPHA = (2 * DEPTH) ** 0.25
LANES = 128
DENSE_ROWS = 1024
DENSE_FF_TILE = 256
EXPERT_FF_TILE = 512
EXPERT_OUT_TILE = 512
EXPERT_TILE_ROWS = 1024
EXPERT_SUB_ROWS = 256
DMA_UNROLL = 8
EPI_COLS = 1024
EPI_ROWS = 128
VMEM_LIMIT = 56 * 1024 * 1024
EXP_UNDERFLOW = -104.0
MASKED_LOGIT = -1e30

BF16 = jnp.bfloat16
F32 = jnp.float32


def _params(n_axes):
    return pltpu.CompilerParams(dimension_semantics=("arbitrary",) * n_axes,
                                vmem_limit_bytes=VMEM_LIMIT)


def _layer_norm(t, g, b):
    mu = jnp.mean(t, axis=-1, keepdims=True)
    c = t - mu
    var = jnp.mean(c * c, axis=-1, keepdims=True)
    return c * lax.rsqrt(var + LN_EPS) * g + b


def _mm_kernel(x_ref, w_ref, o_ref, wb_ref, *, scaled_tiles, scale):
    @pl.when(pl.program_id(1) == 0)
    def _():
        wb_ref[...] = w_ref[...].astype(BF16)

    acc = jnp.dot(x_ref[...], wb_ref[...], preferred_element_type=F32)
    if scaled_tiles:
        acc = acc * jnp.where(pl.program_id(0) < scaled_tiles, scale, 1.0)
    o_ref[...] = acc.astype(o_ref.dtype)


def matmul(x, w, out_dtype, *, tm=1024, tn=512, scaled_cols=0, scale=1.0):
    M, K = x.shape
    N = w.shape[1]
    tm, tn = min(tm, M), min(tn, N)
    assert M % tm == 0 and N % tn == 0 and scaled_cols % tn == 0
    return pl.pallas_call(
        functools.partial(_mm_kernel, scaled_tiles=scaled_cols // tn, scale=scale),
        grid=(N // tn, M // tm),
        in_specs=[pl.BlockSpec((tm, K), lambda j, i: (i, 0)),
                  pl.BlockSpec((K, tn), lambda j, i: (0, j))],
        out_specs=pl.BlockSpec((tm, tn), lambda j, i: (i, j)),
        out_shape=jax.ShapeDtypeStruct((M, N), out_dtype),
        scratch_shapes=[pltpu.VMEM((K, tn), BF16)],
        compiler_params=_params(2),
        name="matmul",
    )(x, w)


def _mm_ln_kernel(x_ref, w_ref, res_hbm, g_ref, b_ref, o_ref, *rest, nk, k_tail):
    obf_ref = rest[0] if len(rest) == 3 else None
    res_ref, res_sem = rest[-2:]
    i, k = pl.program_id(0), pl.program_id(1)
    tm, D = o_ref.shape
    tk = x_ref.shape[1]
    res_copy = pltpu.make_async_copy(res_hbm.at[pl.ds(pl.multiple_of(i * tm, tm), tm), :], res_ref, res_sem)

    @pl.when(k == 0)
    def _():
        res_copy.start()
        o_ref[...] = jnp.zeros_like(o_ref)

    def accumulate(valid):
        x = x_ref[...]
        if valid < tk:
            keep_x = lax.broadcasted_iota(jnp.int32, (1, tk), 1) < valid
            keep_w = lax.broadcasted_iota(jnp.int32, (tk, 1), 0) < valid
            x = jnp.where(keep_x, x.astype(F32), 0.0).astype(BF16)
        for c in range(0, D, EPI_COLS):
            cs = slice(c, c + EPI_COLS)
            w = w_ref[:, cs]
            if valid < tk:
                w = jnp.where(keep_w, w.astype(F32), 0.0)
            o_ref[:, cs] += jnp.dot(x, w.astype(BF16), preferred_element_type=F32)

    if k_tail == tk:
        accumulate(tk)
    else:
        pl.when(k < nk - 1)(lambda: accumulate(tk))
        pl.when(k == nk - 1)(lambda: accumulate(k_tail))

    @pl.when(k == nk - 1)
    def _():
        res_copy.wait()
        for r in range(0, tm, EPI_ROWS):
            rs = slice(r, r + EPI_ROWS)
            y = _layer_norm(ALPHA * res_ref[rs, :] + o_ref[rs, :], g_ref[...], b_ref[...])
            o_ref[rs, :] = y
            if obf_ref is not None:
                obf_ref[rs, :] = y.astype(BF16)


def matmul_residual_ln(x, w, res, g, b, *, emit_bf16, tm=512, tk=512):
    M, K = x.shape
    D = w.shape[1]
    tm = min(tm, M)
    assert M % tm == 0
    nk = pl.cdiv(K, tk)
    row = pl.BlockSpec((tm, D), lambda i, k: (i, 0))
    vec = pl.BlockSpec((1, D), lambda i, k: (0, 0))
    return pl.pallas_call(
        functools.partial(_mm_ln_kernel, nk=nk, k_tail=K - (nk - 1) * tk),
        grid=(M // tm, nk),
        in_specs=[pl.BlockSpec((tm, tk), lambda i, k: (i, k)),
                  pl.BlockSpec((tk, D), lambda i, k: (k, 0)),
                  pl.BlockSpec(memory_space=pl.ANY),
                  vec, vec],
        out_specs=[row, row] if emit_bf16 else [row],
        out_shape=[jax.ShapeDtypeStruct((M, D), F32)] + [jax.ShapeDtypeStruct((M, D), BF16)] * emit_bf16,
        scratch_shapes=[pltpu.VMEM((tm, D), F32), pltpu.SemaphoreType.DMA(())],
        compiler_params=_params(2),
        name="matmul_residual_ln",
    )(x, w, res, g.reshape(1, D), b.reshape(1, D))


def _pool_kernel(u_ref, halo_ref, x_ref, wg_ref, sc_ref, g_ref, b_ref, o_ref, obf_ref, ext_ref, *, tm, C):
    i = pl.program_id(0)
    pos = i * tm + 1 + lax.broadcasted_iota(jnp.int32, (tm, 1), 0)
    for gi, w in enumerate(POOL_WINDOWS):
        cs = slice(gi * C, (gi + 1) * C)
        ext_ref[0:POOL_HALO, :] = jnp.where(i > 0, halo_ref[:, cs], 0.0)
        ext_ref[POOL_HALO:, :] = u_ref[:, cs]
        u = u_ref[:, cs]
        win = u
        for k in range(1, w):
            win = win + ext_ref[pl.ds(POOL_HALO - k, tm), :]
        count = jnp.minimum(pos, w).astype(F32)
        d = win / count - u
        y = jnp.dot(d.astype(BF16), wg_ref[gi], preferred_element_type=F32)
        o_ref[:, cs] = ALPHA * x_ref[:, cs] + y * sc_ref[:, cs]
    for r in range(0, tm, EPI_ROWS):
        rs = slice(r, r + EPI_ROWS)
        y = _layer_norm(o_ref[rs, :], g_ref[...], b_ref[...])
        o_ref[rs, :] = y
        obf_ref[rs, :] = y.astype(BF16)


def pool_mix_ln(u, x, w_group, scale, g, b, *, tm=256):
    S, D = u.shape
    G, C, _ = w_group.shape
    assert G == len(POOL_WINDOWS) and G * C == D and S % tm == 0 and tm % POOL_HALO == 0
    hb = tm // POOL_HALO
    row = pl.BlockSpec((tm, D), lambda i: (i, 0))
    vec = pl.BlockSpec((1, D), lambda i: (0, 0))
    return pl.pallas_call(
        functools.partial(_pool_kernel, tm=tm, C=C),
        grid=(S // tm,),
        in_specs=[row,
                  pl.BlockSpec((POOL_HALO, D), lambda i: (jnp.maximum(i * hb - 1, 0), 0)),
                  row,
                  pl.BlockSpec((G, C, C), lambda i: (0, 0, 0)),
                  vec, vec, vec],
        out_specs=[row, row],
        out_shape=[jax.ShapeDtypeStruct((S, D), F32), jax.ShapeDtypeStruct((S, D), BF16)],
        scratch_shapes=[pltpu.VMEM((tm + POOL_HALO, C), F32)],
        compiler_params=_params(1),
        name="pool_mix_ln",
    )(u, u, x, w_group, scale.reshape(1, D), g.reshape(1, D), b.reshape(1, D))


def _glu_math(x, wg, wu):
    gate = jnp.dot(x, wg, preferred_element_type=F32)
    up = jnp.dot(x, wu, preferred_element_type=F32)
    return (gate * jax.nn.sigmoid(gate) * up).astype(BF16)


def _sb_tile(q, ks, vs, tri2, drop_run, mask):
    tb = tri2.shape[1]
    z = lax.dot_general(q, ks, (((1,), (1,)), ((), ())), preferred_element_type=F32)
    if mask is not None:
        z = jnp.where(mask, z, MASKED_LOGIT)
    sp = jnp.log(1.0 + jnp.exp(-jnp.abs(z)))
    log_beta = jnp.minimum(z, 0.0) - sp
    drop = jnp.maximum(z, 0.0) + sp
    later = [None] * (z.shape[1] // tb)
    for blk in reversed(range(len(later))):
        d = drop[:, blk * tb:(blk + 1) * tb]
        hi = d.astype(BF16)
        lo = (d - hi.astype(F32)).astype(BF16)
        later[blk] = jnp.dot(jnp.concatenate([hi, lo], axis=1), tri2, preferred_element_type=F32) + drop_run
        drop_run = drop_run + jnp.sum(d, axis=1, keepdims=True)
    w = jnp.exp(log_beta - jnp.concatenate(later, axis=1))
    pv = jnp.dot(w.astype(BF16), vs, preferred_element_type=F32)
    return pv, drop_run


def _attn_kernel(q_ref, k_ref, v_ref, tri_ref, o_ref, *, tq, heads):
    i = pl.program_id(1)
    t0 = i * tq
    w0 = pl.multiple_of(jnp.maximum(t0 - tq, 0), tq)
    t_pos = t0 + lax.broadcasted_iota(jnp.int32, (tq, 1), 0)
    mask = (w0 + lax.broadcasted_iota(jnp.int32, (1, 2 * tq), 1)) < t_pos
    zero = jnp.zeros((tq, 1), F32)
    accs, runs = [], []
    for g in range(heads):
        hs = slice(g * HEAD_DIM, (g + 1) * HEAD_DIM)
        pv, run = _sb_tile(q_ref[:, hs], k_ref[pl.ds(w0, 2 * tq), hs], v_ref[pl.ds(w0, 2 * tq), hs],
                           tri_ref[...], zero, mask)
        accs.append(pv)
        runs.append(run)

    def alive(runs):
        return functools.reduce(jnp.minimum, [jnp.min(r) for r in runs]) < -EXP_UNDERFLOW

    def cond(carry):
        j, runs, _ = carry
        return jnp.logical_and(j >= 0, alive(runs))

    def body(carry):
        j, runs, accs = carry
        start = pl.multiple_of(j * tq, tq)
        new_runs, new_accs = [], []
        for g in range(heads):
            hs = slice(g * HEAD_DIM, (g + 1) * HEAD_DIM)
            pv, run = _sb_tile(q_ref[:, hs], k_ref[pl.ds(start, tq), hs], v_ref[pl.ds(start, tq), hs],
                               tri_ref[...], runs[g], None)
            new_accs.append(accs[g] + pv)
            new_runs.append(run)
        return j - 1, tuple(new_runs), tuple(new_accs)

    _, _, accs = lax.while_loop(cond, body, (w0 // tq - 1, tuple(runs), tuple(accs)))
    for g in range(heads):
        o_ref[:, g * HEAD_DIM:(g + 1) * HEAD_DIM] = accs[g].astype(o_ref.dtype)


def _stacked_tri(n):
    r = lax.broadcasted_iota(jnp.int32, (n, n), 0)
    c = lax.broadcasted_iota(jnp.int32, (n, n), 1)
    tri = (r > c).astype(BF16)
    return jnp.concatenate([tri, tri], axis=0)


def stick_breaking_attention(qkv, *, tq=256, heads=4):
    S, D3 = qkv.shape
    D = D3 // 3
    H = D // HEAD_DIM
    assert S % tq == 0 and S >= 2 * tq and H % heads == 0
    hw = heads * HEAD_DIM
    nb = D // hw
    return pl.pallas_call(
        functools.partial(_attn_kernel, tq=tq, heads=heads),
        grid=(nb, S // tq),
        in_specs=[pl.BlockSpec((tq, hw), lambda h, i: (i, h)),
                  pl.BlockSpec((S, hw), lambda h, i: (0, nb + h)),
                  pl.BlockSpec((S, hw), lambda h, i: (0, 2 * nb + h)),
                  pl.BlockSpec((2 * tq, tq), lambda h, i: (0, 0))],
        out_specs=pl.BlockSpec((tq, hw), lambda h, i: (i, h)),
        out_shape=jax.ShapeDtypeStruct((S, D), BF16),
        compiler_params=_params(2),
        name="stick_breaking_attention",
    )(qkv, qkv, qkv, _stacked_tri(tq))


def _router_kernel(h_ref, w_ref, o_ref, *, n_experts):
    h = h_ref[...]
    w = w_ref[...]
    h1 = h.astype(BF16)
    h2 = (h - h1.astype(F32)).astype(BF16)
    h3 = (h - h1.astype(F32) - h2.astype(F32)).astype(BF16)
    w1 = w.astype(BF16)
    w2 = (w - w1.astype(F32)).astype(BF16)
    w3 = (w - w1.astype(F32) - w2.astype(F32)).astype(BF16)
    dot = functools.partial(jnp.dot, preferred_element_type=F32)
    logits = (dot(h3, w1) + dot(h2, w2) + dot(h1, w3)) + (dot(h2, w1) + dot(h1, w2)) + dot(h1, w1)
    lane = lax.broadcasted_iota(jnp.int32, logits.shape, 1)
    lane_f = lane.astype(F32)
    neg = jnp.float32(-jnp.inf)
    logits = jnp.where(lane < n_experts, logits, neg)
    m1 = jnp.max(logits, axis=1, keepdims=True)
    i1 = jnp.min(jnp.where(logits == m1, lane_f, float(LANES)), axis=1, keepdims=True)
    rest = jnp.where(lane_f == i1, neg, logits)
    m2 = jnp.max(rest, axis=1, keepdims=True)
    i2 = jnp.min(jnp.where(rest == m2, lane_f, float(LANES)), axis=1, keepdims=True)
    e2 = jnp.exp(m2 - m1)
    g1 = 1.0 / (1.0 + e2)
    g2 = e2 / (1.0 + e2)
    out = jnp.where(lane == 0, i1, 0.0)
    out = jnp.where(lane == 1, i2, out)
    out = jnp.where(lane == 2, g1, out)
    out = jnp.where(lane == 3, g2, out)
    o_ref[...] = out


def route_top2(h, w_router, *, tm=256):
    S, D = h.shape
    E = w_router.shape[1]
    assert E <= LANES and S % tm == 0
    w_pad = jnp.pad(w_router, ((0, 0), (0, LANES - E)))
    info = pl.pallas_call(
        functools.partial(_router_kernel, n_experts=E),
        grid=(S // tm,),
        in_specs=[pl.BlockSpec((tm, D), lambda i: (i, 0)),
                  pl.BlockSpec((D, LANES), lambda i: (0, 0))],
        out_specs=pl.BlockSpec((tm, LANES), lambda i: (i, 0)),
        out_shape=jax.ShapeDtypeStruct((S, LANES), F32),
        compiler_params=_params(1),
        name="route_top2",
    )(h, w_pad)
    return info[:, 0:TOP_K].astype(jnp.int32), info[:, TOP_K:2 * TOP_K]


def _gather_kernel(src_ref, live_ref, h_ref, o_ref, stage_ref, sems, *, tg):
    i, n = pl.program_id(0), pl.num_programs(0)
    slot = i % 2

    def row_copy(src_row, r, slot):
        return pltpu.make_async_copy(h_ref.at[pl.ds(src_row, 1), :],
                                     stage_ref.at[slot, pl.ds(r, 1), :], sems.at[slot])

    def issue(step, slot):
        def start_row(r, c):
            row_copy(src_ref[step * tg + r], r, slot).start()
            return c
        lax.fori_loop(0, tg, start_row, 0, unroll=DMA_UNROLL)

    @pl.when(jnp.logical_and(i == 0, live_ref[0] == 1))
    def _():
        issue(0, 0)

    @pl.when(jnp.logical_and(i + 1 < n, live_ref[jnp.minimum(i + 1, n - 1)] == 1))
    def _():
        issue(i + 1, 1 - slot)

    @pl.when(live_ref[i] == 1)
    def _():
        def wait_row(r, c):
            row_copy(0, r, slot).wait()
            return c
        lax.fori_loop(0, tg, wait_row, 0, unroll=DMA_UNROLL)
        o_ref[...] = stage_ref[slot].astype(BF16)

    @pl.when(live_ref[i] == 0)
    def _():
        o_ref[...] = jnp.zeros_like(o_ref)


def gather_rows_bf16(h, src_rows, tile_live, *, tg):
    S, D = h.shape
    N = src_rows.shape[0]
    assert N % tg == 0 and tile_live.shape[0] == N // tg
    return pl.pallas_call(
        functools.partial(_gather_kernel, tg=tg),
        grid_spec=pltpu.PrefetchScalarGridSpec(
            num_scalar_prefetch=2,
            grid=(N // tg,),
            in_specs=[pl.BlockSpec(memory_space=pl.ANY)],
            out_specs=pl.BlockSpec((tg, D), lambda i, src, live: (i, 0)),
            scratch_shapes=[pltpu.VMEM((2, tg, D), F32), pltpu.SemaphoreType.DMA((2,))]),
        out_shape=jax.ShapeDtypeStruct((N, D), BF16),
        compiler_params=_params(1),
        name="gather_rows_bf16",
    )(src_rows, tile_live, h)


def _run_schedule(tile_expert, used):
    T = tile_expert.shape[0]
    t = jnp.arange(T, dtype=jnp.int32)
    prev = jnp.concatenate([tile_expert[:1], tile_expert[:-1]])
    first = jnp.logical_and(jnp.logical_or(t == 0, tile_expert != prev), t < used[0])
    run = jnp.cumsum(first.astype(jnp.int32)) - 1
    later_first = jnp.logical_and(first[None, :], t[None, :] > t[:, None])
    nxt_tile = jnp.min(jnp.where(later_first, t[None, :], T), axis=1)
    nxt = jnp.where(nxt_tile < T, tile_expert[jnp.minimum(nxt_tile, T - 1)], tile_expert[0])
    counts = jnp.stack([used[0], jnp.sum(first.astype(jnp.int32))])
    return first.astype(jnp.int32), run.astype(jnp.int32), nxt.astype(jnp.int32), counts.astype(jnp.int32)


def _stream_run_weights(sched, copies, on_arrival):
    te_ref, first_ref, run_ref, nxt_ref, cnt_ref = sched
    c, t, nc = pl.program_id(0), pl.program_id(1), pl.num_programs(0)
    n_runs = cnt_ref[1]

    @pl.when(first_ref[t] == 1)
    def _():
        @pl.when(jnp.logical_and(c == 0, run_ref[t] == 0))
        def _():
            for cp in copies(te_ref[t], c):
                cp.start()

        for cp in copies(te_ref[t], c):
            cp.wait()
        on_arrival()
        next_c = c + (run_ref[t] == n_runs - 1).astype(jnp.int32)

        @pl.when(next_c < nc)
        def _():
            for cp in copies(nxt_ref[t], next_c):
                cp.start()


def _grouped_kernel(te_ref, first_ref, run_ref, nxt_ref, cnt_ref, valid_ref, x_ref, *rest, n_w, sub, math):
    w_hbm, o_ref = rest[:n_w], rest[n_w]
    stage_ref, wb_ref, sems = rest[n_w + 1:]
    t = pl.program_id(1)
    tm, tc = o_ref.shape

    def copies(e, c):
        cols = pl.ds(pl.multiple_of(c * tc, tc), tc)
        return [pltpu.make_async_copy(w_hbm[k].at[e, :, cols], stage_ref.at[k], sems.at[k])
                for k in range(n_w)]

    def cast():
        for k in range(n_w):
            wb_ref[k] = stage_ref[k].astype(BF16)

    _stream_run_weights((te_ref, first_ref, run_ref, nxt_ref, cnt_ref), copies, cast)

    for v in range(tm // sub + 1):
        rows = v * sub

        @pl.when(valid_ref[t] == v)
        def _():
            if rows:
                o_ref[:rows, :] = math(x_ref[:rows, :], *[wb_ref[k] for k in range(n_w)])
            if rows < tm:
                o_ref[rows:, :] = jnp.zeros((tm - rows, tc), o_ref.dtype)


def grouped_matmul(x, ws, tile_expert, used, valid, math, out_dtype, *, tm, sub, tc):
    N, K = x.shape
    E, _, C = ws[0].shape
    n_w = len(ws)
    assert N % tm == 0 and C % tc == 0 and tm % sub == 0
    sched = (tile_expert,) + _run_schedule(tile_expert, used) + (valid,)
    return pl.pallas_call(
        functools.partial(_grouped_kernel, n_w=n_w, sub=sub, math=math),
        grid_spec=pltpu.PrefetchScalarGridSpec(
            num_scalar_prefetch=len(sched),
            grid=(C // tc, N // tm),
            in_specs=[pl.BlockSpec((tm, K), lambda c, t, te, first, run, nxt, cnt, valid:
                                   (jnp.minimum(t, cnt[0] - 1), 0))]
                     + [pl.BlockSpec(memory_space=pl.ANY)] * n_w,
            out_specs=pl.BlockSpec((tm, tc), lambda c, t, *_: (t, c)),
            scratch_shapes=[pltpu.VMEM((n_w, K, tc), F32), pltpu.VMEM((n_w, K, tc), BF16),
                            pltpu.SemaphoreType.DMA((n_w,))]),
        out_shape=jax.ShapeDtypeStruct((N, C), out_dtype),
        compiler_params=_params(2),
        name="grouped_matmul",
    )(*sched, x, *ws)


def _down_math(x, w):
    return jnp.dot(x, w, preferred_element_type=F32)


def _combine_kernel(pos_ref, y_ref, h_ref, gate_ref, g_ref, b_ref, o_ref, stage_ref, sems, *, tc):
    i = pl.program_id(0)
    n = pl.num_programs(0)
    slot = i % 2

    def row_copy(y_row, r, k, slot):
        return pltpu.make_async_copy(y_ref.at[pl.ds(y_row, 1), :],
                                     stage_ref.at[slot, k, pl.ds(r, 1), :], sems.at[slot])

    def issue(step, slot):
        def start_row(r, c):
            for k in range(TOP_K):
                row_copy(pos_ref[(step * tc + r) * TOP_K + k], r, k, slot).start()
            return c
        lax.fori_loop(0, tc, start_row, 0, unroll=DMA_UNROLL)

    @pl.when(i == 0)
    def _():
        issue(0, 0)

    @pl.when(i + 1 < n)
    def _():
        issue(i + 1, 1 - slot)

    def wait_row(r, c):
        for k in range(TOP_K):
            row_copy(0, r, k, slot).wait()
        return c
    lax.fori_loop(0, tc, wait_row, 0, unroll=DMA_UNROLL)

    for r in range(0, tc, EPI_ROWS):
        rs = slice(r, r + EPI_ROWS)
        gates = gate_ref[rs, :]
        moe = gates[:, 0:1] * stage_ref[slot, 0, rs, :]
        for k in range(1, TOP_K):
            moe = moe + gates[:, k:k + 1] * stage_ref[slot, k, rs, :]
        o_ref[rs, :] = _layer_norm(ALPHA * h_ref[rs, :] + moe, g_ref[...], b_ref[...])


def combine_ln(y, pos, h, gates, g, b, *, tc=256):
    S, D = h.shape
    assert S % tc == 0
    row = pl.BlockSpec((tc, D), lambda i, pos: (i, 0))
    vec = pl.BlockSpec((1, D), lambda i, pos: (0, 0))
    return pl.pallas_call(
        functools.partial(_combine_kernel, tc=tc),
        grid_spec=pltpu.PrefetchScalarGridSpec(
            num_scalar_prefetch=1,
            grid=(S // tc,),
            in_specs=[pl.BlockSpec(memory_space=pl.ANY), row,
                      pl.BlockSpec((tc, TOP_K), lambda i, pos: (i, 0)), vec, vec],
            out_specs=row,
            scratch_shapes=[pltpu.VMEM((2, TOP_K, tc, D), F32), pltpu.SemaphoreType.DMA((2,))]),
        out_shape=jax.ShapeDtypeStruct((S, D), F32),
        compiler_params=_params(1),
        name="combine_ln",
    )(pos.reshape(-1), y, h, gates, g.reshape(1, D), b.reshape(1, D))


def _routing_tables(idx, n_experts, tile, sub):
    S, K = idx.shape
    flat = idx.reshape(-1)
    onehot = (flat[:, None] == jnp.arange(n_experts, dtype=jnp.int32)[None, :]).astype(jnp.int32)
    csum = jnp.cumsum(onehot, axis=0)
    rank = jnp.sum(onehot * csum, axis=1) - 1
    counts = csum[-1]
    padded = (counts + tile - 1) // tile * tile
    ends = jnp.cumsum(padded)
    starts = ends - padded
    pos = jnp.sum(onehot * starts[None, :], axis=1) + rank
    n_rows = S * K + n_experts * tile
    src = jnp.zeros((n_rows,), jnp.int32).at[pos].set(jnp.arange(S * K, dtype=jnp.int32) // K)
    used = ends[-1] // tile
    t = jnp.arange(n_rows // tile, dtype=jnp.int32)
    tile_expert = jnp.sum((jnp.minimum(t, used - 1)[:, None] * tile >= ends[None, :]).astype(jnp.int32), axis=1)
    rows_in_tile = jnp.clip(counts[tile_expert] - (t * tile - starts[tile_expert]), 0, tile)
    valid = jnp.where(t < used, (rows_in_tile + sub - 1) // sub, 0)
    per = tile // sub
    sub_live = (jnp.arange(n_rows // sub, dtype=jnp.int32) % per) < jnp.repeat(valid, per)
    i32 = lambda a: a.astype(jnp.int32)
    return pos.reshape(S, K), src, i32(tile_expert), i32(used.reshape(1)), i32(valid), i32(sub_live)


def moe_ln(h, w_router, wg, wu, wd, g, b):
    E = w_router.shape[1]
    tm, sub = EXPERT_TILE_ROWS, EXPERT_SUB_ROWS
    idx, gates = route_top2(h, w_router)
    pos, src, tile_expert, used, valid, sub_live = _routing_tables(idx, E, tm, sub)
    xg = gather_rows_bf16(h, src, sub_live, tg=sub)
    hid = grouped_matmul(xg, (wg, wu), tile_expert, used, valid, _glu_math, BF16,
                         tm=tm, sub=sub, tc=EXPERT_FF_TILE)
    y = grouped_matmul(hid, (wd,), tile_expert, used, valid, _down_math, F32,
                       tm=tm, sub=sub, tc=EXPERT_OUT_TILE)
    return combine_ln(y, pos, h, gates, g, b)


def kernel(x, l0_pool_w_in, l0_pool_w_group, l0_pool_scale, l0_ln1_g, l0_ln1_b, l0_ffn_w_gate, l0_ffn_w_up, l0_ffn_w_down, l0_ln2_g, l0_ln2_b, l1_attn_w_qkv, l1_attn_w_o, l1_ln1_g, l1_ln1_b, l1_moe_w_router, l1_moe_w_gate, l1_moe_w_up, l1_moe_w_down, l1_ln2_g, l1_ln2_b):
    B, S, D = x.shape
    x0 = x.reshape(B * S, D)
    assert B == 1, "sequence mixing below treats all rows as one sequence"
    bf = lambda w: w.astype(BF16)

    u = matmul(bf(x0), l0_pool_w_in, F32)
    x1, x1b = pool_mix_ln(u, x0, bf(l0_pool_w_group), l0_pool_scale, l0_ln1_g, l0_ln1_b)
    n_tiles = (B * S) // DENSE_ROWS
    hid = grouped_matmul(x1b, (l0_ffn_w_gate[None], l0_ffn_w_up[None]), jnp.zeros((n_tiles,), jnp.int32),
                         jnp.full((1,), n_tiles, jnp.int32), jnp.ones((n_tiles,), jnp.int32), _glu_math, BF16,
                         tm=DENSE_ROWS, sub=DENSE_ROWS, tc=DENSE_FF_TILE)
    x2, x2b = matmul_residual_ln(hid, bf(l0_ffn_w_down), x1, l0_ln2_g, l0_ln2_b, emit_bf16=True)

    qkv = matmul(x2b, l1_attn_w_qkv, BF16, scaled_cols=D, scale=HEAD_DIM ** -0.5)
    o = stick_breaking_attention(qkv)
    (x3,) = matmul_residual_ln(o, l1_attn_w_o, x2, l1_ln1_g, l1_ln1_b, emit_bf16=False)
    x4 = moe_ln(x3, l1_moe_w_router, l1_moe_w_gate, l1_moe_w_up, l1_moe_w_down, l1_ln2_g, l1_ln2_b)
    return x4.reshape(B, S, D)
```
